```python
import math
import jax, jax.numpy as jnp
from jax import lax
import numpy as np

D_MODEL = 1024
BATCH = 4
SEQ = 4096
DEPTH = 4
DEC_BATCH = 128
DEC_SEQ = 8
PAST_LEN = 2048
PAGE_SIZE = 128

HA = 4
DHA = D_MODEL // 16
HG = 4
DKG = D_MODEL // 16
DVG = D_MODEL // 8
GATE_RANK = 16
GATE_NORM = 16.0
GLA_CHUNK = 16
N_BUCKETS = 32
MAX_DISTANCE = 128
Q_BLOCK = 128
FFN_HIDDEN = -(-8 * D_MODEL // (3 * 256)) * 256
EPS = 1e-6
SPLIT_WIDTHS = (HA * 2 * DHA, HA * 2 * DHA, HA * 2 * DHA, HG * DKG, HG * DKG, HG * DVG, HG * DVG, GATE_RANK, 2 * D_MODEL)
PROJ_WIDTH = sum(SPLIT_WIDTHS)

kernel_name = "diffattn_gla_gated_hybrid_step"


def rms_norm(x, gain):
    xf = x.astype(jnp.float32)
    y = xf * lax.rsqrt(jnp.mean(xf * xf, axis=-1, keepdims=True) + EPS)
    return (y * gain.astype(jnp.float32)).astype(x.dtype)


def t5_bucket(dist):
    n = jnp.maximum(dist, 0)
    max_exact = N_BUCKETS // 2
    nf = jnp.maximum(n, 1).astype(jnp.float32)
    large = max_exact + (jnp.log(nf / max_exact) / math.log(MAX_DISTANCE / max_exact) * (N_BUCKETS - max_exact)).astype(jnp.int32)
    large = jnp.minimum(large, N_BUCKETS - 1)
    return jnp.where(n < max_exact, n, large)


def project(h, w_in, w_alpha2, b_alpha, qn_gain, kn_gain):
    B, T, _ = h.shape
    proj = jnp.einsum('btd,df->btf', h, w_in)
    cuts = [int(c) for c in np.cumsum(SPLIT_WIDTHS)[:-1]]
    qa, ka, va, qg, kg, vg, rg, a_lr, gate_logits = jnp.split(proj, cuts, axis=-1)
    qa = rms_norm(qa.reshape(B, T, HA, 2, DHA), qn_gain) * (DHA ** -0.5)
    ka = rms_norm(ka.reshape(B, T, HA, 2, DHA), kn_gain)
    va = va.reshape(B, T, HA, 2 * DHA)
    qg = qg.reshape(B, T, HG, DKG) * (DKG ** -0.5)
    kg = kg.reshape(B, T, HG, DKG)
    vg = vg.reshape(B, T, HG, DVG)
    log_a = jax.nn.log_sigmoid((jnp.einsum('btr,rf->btf', a_lr, w_alpha2) + b_alpha).astype(jnp.float32)) / GATE_NORM
    log_a = log_a.reshape(B, T, HG, DKG)
    return qa, ka, va, qg, kg, vg, rg, log_a, gate_logits


def diff_lambda(lq1, lk1, lq2, lk2, lam_init):
    f = lambda a: a.astype(jnp.float32)
    return jnp.exp(jnp.sum(f(lq1) * f(lk1))) - jnp.exp(jnp.sum(f(lq2) * f(lk2))) + lam_init


def diff_attn_core(q, k, v, q_pos, k_pos, rel_bias, lam):
    logits = jnp.einsum('bqhcd,bkhcd->bhcqk', q, k, preferred_element_type=jnp.float32)
    dist = q_pos[:, None] - k_pos[None, :]
    bias = jnp.moveaxis(rel_bias[t5_bucket(dist)], -1, 0)
    logits = logits + bias.astype(jnp.float32)[None, :, None]
    logits = jnp.where(dist >= 0, logits, -jnp.inf)
    p = jax.nn.softmax(logits, axis=-1)
    attn = p[:, :, 0] - lam * p[:, :, 1]
    return jnp.einsum('bhqk,bkhe->bqhe', attn.astype(v.dtype), v)


def diff_attention_prompt(q, k, v, rel_bias, lam):
    B, T = q.shape[:2]
    nb = T // Q_BLOCK
    qb = jnp.moveaxis(q.reshape(B, nb, Q_BLOCK, HA, 2, DHA), 1, 0)
    starts = jnp.arange(nb, dtype=jnp.int32) * Q_BLOCK
    k_pos = jnp.arange(T, dtype=jnp.int32)

    def block(args):
        q_blk, start = args
        return diff_attn_core(q_blk, k, v, start + jnp.arange(Q_BLOCK, dtype=jnp.int32), k_pos, rel_bias, lam)

    o = lax.map(block, (qb, starts))
    return jnp.moveaxis(o, 0, 1).reshape(B, T, HA, 2 * DHA)


def gla_chunked(q, k, v, log_a, S0):
    B, T = q.shape[:2]
    q, k, v = (a.astype(jnp.float32) for a in (q, k, v))
    pad = (-T) % GLA_CHUNK
    if pad:
        pw = ((0, 0), (0, pad), (0, 0), (0, 0))
        q, k, v, log_a = (jnp.pad(a, pw) for a in (q, k, v, log_a))
    n = (T + pad) // GLA_CHUNK

    def to_chunks(a):
        return jnp.moveaxis(a.reshape(B, n, GLA_CHUNK, *a.shape[2:]), 1, 0)

    mask = jnp.tril(jnp.ones((GLA_CHUNK, GLA_CHUNK), bool))[None, :, :, None, None]

    def step(S, xs):
        qc, kc, vc, gc = xs
        b = jnp.cumsum(gc, axis=1)
        o_inter = jnp.einsum('bthk,bhkv->bthv', qc * jnp.exp(b), S)
        rel = jnp.where(mask, b[:, :, None] - b[:, None, :], -jnp.inf)
        A = jnp.einsum('bthk,bshk,btshk->bths', qc, kc, jnp.exp(rel))
        o_intra = jnp.einsum('bths,bshv->bthv', A, vc)
        b_last = b[:, -1]
        k_dec = kc * jnp.exp(b_last[:, None] - b)
        S_new = jnp.exp(b_last)[..., None] * S + jnp.einsum('bshk,bshv->bhkv', k_dec, vc)
        return S_new, o_inter + o_intra

    S, o = lax.scan(step, S0, (to_chunks(q), to_chunks(k), to_chunks(v), to_chunks(log_a)))
    o = jnp.moveaxis(o, 0, 1).reshape(B, n * GLA_CHUNK, HG, DVG)[:, :T]
    return o, S


def merge(oa, og, rg, gate_logits, lam_init, subln, gla_norm, w_diff_out, w_gla_out, w_out):
    B, T = rg.shape[:2]
    ya = rms_norm(oa, subln) * (1.0 - lam_init)
    yg = rms_norm(og.astype(rg.dtype), gla_norm) * jax.nn.silu(rg).reshape(B, T, HG, DVG)
    ya = jnp.einsum('bte,ed->btd', ya.reshape(B, T, -1), w_diff_out)
    yg = jnp.einsum('bte,ed->btd', yg.reshape(B, T, -1), w_gla_out)
    ga, gg = jnp.split(jax.nn.sigmoid(gate_logits), 2, axis=-1)
    return jnp.einsum('btd,de->bte', ga * ya + gg * yg, w_out)


def swiglu(h, wg, wu, wd):
    return jnp.einsum('btf,fd->btd', jax.nn.silu(h @ wg) * (h @ wu), wd)


def setup_inputs(seed: int = 0) -> dict:
    key = jax.random.key(seed)
    ks = jax.random.split(key, 32)
    n_pages = PAST_LEN // PAGE_SIZE
    n_used = DEC_BATCH * n_pages
    n_pool = n_used + -(-n_used // 4)
    nrm = lambda k, shape, s: jax.random.normal(k, shape, jnp.float32) * s
    page_table = jax.random.permutation(ks[0], n_pool)[:n_used].reshape(DEC_BATCH, n_pages).astype(jnp.int32)
    return {
        "x_prompt": nrm(ks[1], (BATCH, SEQ, D_MODEL), 1.0),
        "x_sample": nrm(ks[2], (DEC_BATCH, DEC_SEQ, D_MODEL), 1.0),
        "cache_k": nrm(ks[3], (DEPTH, n_pool, PAGE_SIZE, HA, 2, DHA), 1.0),
        "cache_v": nrm(ks[4], (DEPTH, n_pool, PAGE_SIZE, HA, 2 * DHA), 1.0),
        "state_gla": nrm(ks[5], (DEPTH, DEC_BATCH, HG, DKG, DVG), 1.0),
        "page_table": page_table,
        "rel_bias": nrm(ks[6], (N_BUCKETS, HA), 0.5),
        "norm_mix": 1.0 + nrm(ks[7], (DEPTH, D_MODEL), 0.02),
        "w_in": nrm(ks[8], (DEPTH, D_MODEL, PROJ_WIDTH), D_MODEL ** -0.5),
        "w_alpha2": nrm(ks[9], (DEPTH, GATE_RANK, HG * DKG), GATE_RANK ** -0.5),
        "b_alpha": nrm(ks[10], (DEPTH, HG * DKG), 0.1),
        "qn_gain": 1.0 + nrm(ks[11], (DEPTH, DHA), 0.02),
        "kn_gain": 1.0 + nrm(ks[12], (DEPTH, DHA), 0.02),
        "lam_q1": nrm(ks[13], (DEPTH, DHA), 0.1),
        "lam_k1": nrm(ks[14], (DEPTH, DHA), 0.1),
        "lam_q2": nrm(ks[15], (DEPTH, DHA), 0.1),
        "lam_k2": nrm(ks[16], (DEPTH, DHA), 0.1),
        "subln_gain": 1.0 + nrm(ks[17], (DEPTH, 2 * DHA), 0.02),
        "gla_norm_gain": 1.0 + nrm(ks[18], (DEPTH, DVG), 0.02),
        "w_diff_out": nrm(ks[19], (DEPTH, HA * 2 * DHA, D_MODEL), (HA * 2 * DHA) ** -0.5),
        "w_gla_out": nrm(ks[20], (DEPTH, HG * DVG, D_MODEL), (HG * DVG) ** -0.5),
        "w_out": nrm(ks[21], (DEPTH, D_MODEL, D_MODEL), D_MODEL ** -0.5),
        "norm_ffn": 1.0 + nrm(ks[22], (DEPTH, D_MODEL), 0.02),
        "w_ffn_gate": nrm(ks[23], (DEPTH, D_MODEL, FFN_HIDDEN), D_MODEL ** -0.5),
        "w_ffn_up": nrm(ks[24], (DEPTH, D_MODEL, FFN_HIDDEN), D_MODEL ** -0.5),
        "w_ffn_down": nrm(ks[25], (DEPTH, FFN_HIDDEN, D_MODEL), FFN_HIDDEN ** -0.5),
    }


def reference(x_prompt, x_sample, cache_k, cache_v, state_gla, page_table, rel_bias, norm_mix, w_in, w_alpha2, b_alpha, qn_gain, kn_gain, lam_q1, lam_k1, lam_q2, lam_k2, subln_gain, gla_norm_gain, w_diff_out, w_gla_out, w_out, norm_ffn, w_ffn_gate, w_ffn_up, w_ffn_down):
    xp, xs = x_prompt, x_sample
    B = x_prompt.shape[0]
    DB, TS = x_sample.shape[:2]
    past_len = page_table.shape[1] * PAGE_SIZE
    k_pos_s = jnp.arange(past_len + TS, dtype=jnp.int32)
    q_pos_s = past_len + jnp.arange(TS, dtype=jnp.int32)
    S0p = jnp.zeros((B, HG, DKG, DVG), jnp.float32)
    kp_l, vp_l, sp_l, ks_l, vs_l, ss_l = [], [], [], [], [], []
    for l in range(DEPTH):
        lam_init = 0.8 - 0.6 * math.exp(-0.3 * l)
        lam = diff_lambda(lam_q1[l], lam_k1[l], lam_q2[l], lam_k2[l], lam_init)
        h = rms_norm(xp, norm_mix[l])
        qa, ka, va, qg, kg, vg, rg, la, gl = project(h, w_in[l], w_alpha2[l], b_alpha[l], qn_gain[l], kn_gain[l])
        oa = diff_attention_prompt(qa, ka, va, rel_bias, lam)
        og, Sp = gla_chunked(qg, kg, vg, la, S0p)
        xp = xp + merge(oa, og, rg, gl, lam_init, subln_gain[l], gla_norm_gain[l], w_diff_out[l], w_gla_out[l], w_out[l])
        xp = xp + swiglu(rms_norm(xp, norm_ffn[l]), w_ffn_gate[l], w_ffn_up[l], w_ffn_down[l])
        kp_l.append(ka)
        vp_l.append(va)
        sp_l.append(Sp.astype(state_gla.dtype))
        h = rms_norm(xs, norm_mix[l])
        qa, ka, va, qg, kg, vg, rg, la, gl = project(h, w_in[l], w_alpha2[l], b_alpha[l], qn_gain[l], kn_gain[l])
        past_k = cache_k[l][page_table].reshape(DB, past_len, HA, 2, DHA).astype(ka.dtype)
        past_v = cache_v[l][page_table].reshape(DB, past_len, HA, 2 * DHA).astype(va.dtype)
        k_all = jnp.concatenate([past_k, ka], axis=1)
        v_all = jnp.concatenate([past_v, va], axis=1)
        oa = diff_attn_core(qa, k_all, v_all, q_pos_s, k_pos_s, rel_bias, lam)
        og, Ss = gla_chunked(qg, kg, vg, la, state_gla[l].astype(jnp.float32))
        xs = xs + merge(oa, og, rg, gl, lam_init, subln_gain[l], gla_norm_gain[l], w_diff_out[l], w_gla_out[l], w_out[l])
        xs = xs + swiglu(rms_norm(xs, norm_ffn[l]), w_ffn_gate[l], w_ffn_up[l], w_ffn_down[l])
        ks_l.append(ka)
        vs_l.append(va)
        ss_l.append(Ss.astype(state_gla.dtype))
    k_prompt = jnp.stack(kp_l)
    v_prompt = jnp.stack(vp_l)
    gla_prompt = jnp.stack(sp_l)
    k_sample = jnp.stack(ks_l)
    v_sample = jnp.stack(vs_l)
    gla_sample = jnp.stack(ss_l)
    return (xp, xs, k_prompt, v_prompt, gla_prompt, k_sample, v_sample, gla_sample)
```

```python
import functools
import math

import jax
import jax.numpy as jnp
from jax import lax
from jax.experimental import pallas as pl
from jax.experimental.pallas import tpu as pltpu

F32 = jnp.float32
BF16 = jnp.bfloat16

D_MODEL = 1024
DEPTH = 4
HA = 4
DHA = 64
HG = 4
DKG = 64
DVG = 128
GATE_RANK = 16
GATE_NORM = 16.0
N_BUCKETS = 32
MAX_DISTANCE = 128
PAGE_SIZE = 128
EPS = 1e-6
WA = HA * 2 * DHA
WGK = HG * DKG
WGV = HG * DVG
MAIN_W = 3 * WA + 2 * WGK + 2 * WGV
LANES = 128
NEG = -1e30

TM = 512
TQ = 256
TK = 512
GLA_C = 64
GLA_SB = 8
VMEM_LIMIT = 56 * 1024 * 1024


def _cparams(sem):
    return pltpu.CompilerParams(dimension_semantics=sem, vmem_limit_bytes=VMEM_LIMIT)


def _rms(x, gain):
    ms = jnp.mean(x * x, axis=-1, keepdims=True)
    return x * lax.rsqrt(ms + EPS) * gain


def _dot(a, b):
    return jnp.dot(a, b, preferred_element_type=F32)


def _dot_nt(a, b):
    return lax.dot_general(a, b, (((1,), (1,)), ((), ())), preferred_element_type=F32)


def _dot_tn(a, b):
    return lax.dot_general(a, b, (((0,), (0,)), ((), ())), preferred_element_type=F32)


def _lam(lamp_ref, lam_init):
    a = jnp.sum(lamp_ref[0:1, :] * lamp_ref[1:2, :], axis=-1, keepdims=True)
    b = jnp.sum(lamp_ref[2:3, :] * lamp_ref[3:4, :], axis=-1, keepdims=True)
    return jnp.exp(a) - jnp.exp(b) + lam_init


def _bias_kernel(rb_ref, o_ref, *, period, off0, off_step):
    h = pl.program_id(0)
    t = pl.program_id(1)
    rows, cols = o_ref.shape
    r = lax.broadcasted_iota(jnp.int32, (rows, cols), 0)
    c = lax.broadcasted_iota(jnp.int32, (rows, cols), 1)
    d = off0 + t * off_step + jnp.bitwise_and(r, period - 1) - c
    n = jnp.maximum(d, 0)
    max_exact = N_BUCKETS // 2
    nf = jnp.maximum(n, 1).astype(F32)
    large = max_exact + (jnp.log(nf / max_exact) / math.log(MAX_DISTANCE / max_exact)
                         * (N_BUCKETS - max_exact)).astype(jnp.int32)
    large = jnp.minimum(large, N_BUCKETS - 1)
    bucket = jnp.where(n < max_exact, n, large)
    val = jnp.zeros((rows, cols), F32)
    for k in range(N_BUCKETS):
        val = jnp.where(bucket == k, rb_ref[k, h], val)
    o_ref[...] = jnp.where(d >= 0, val, NEG)


def _bias_tables(rel_bias, n_t, rows, cols, period, off0, off_step):
    return pl.pallas_call(
        functools.partial(_bias_kernel, period=period, off0=off0, off_step=off_step),
        grid=(HA, n_t),
        in_specs=[pl.BlockSpec(memory_space=pltpu.SMEM)],
        out_specs=pl.BlockSpec((None, None, rows, cols), lambda h, t: (h, t, 0, 0)),
        out_shape=jax.ShapeDtypeStruct((HA, n_t, rows, cols), F32),
        compiler_params=_cparams(("arbitrary", "arbitrary")),
        name="bias_tables",
    )(rel_bias)


def _inproj_kernel(x_ref, gmix_ref, w_ref, walr_ref, wa2_ref, ba_ref, gq_ref, gk_ref, gsum_ref,
                   qa_ref, ka_ref, va_ref, qg_ref, kg_ref, vg_ref, rg_ref, la_ref):
    h = _rms(x_ref[...], gmix_ref[...]).astype(BF16)
    proj = _dot(h, w_ref[...])
    gs = gsum_ref[...]
    qa = proj[:, 0:WA]
    ka = proj[:, WA:2 * WA]
    qms = _dot((qa * qa).astype(BF16), gs)
    kms = _dot((ka * ka).astype(BF16), gs)
    qa_ref[...] = (qa * lax.rsqrt(qms + EPS) * gq_ref[...] * (DHA ** -0.5)).astype(BF16)
    ka_ref[...] = ka * lax.rsqrt(kms + EPS) * gk_ref[...]
    va_ref[...] = proj[:, 2 * WA:3 * WA]
    o = 3 * WA
    qg_ref[...] = proj[:, o:o + WGK] * (DKG ** -0.5)
    kg_ref[...] = proj[:, o + WGK:o + 2 * WGK]
    o += 2 * WGK
    vg_ref[...] = proj[:, o:o + WGV]
    rg_ref[...] = proj[:, o + WGV:o + 2 * WGV]
    alr = _dot(h, walr_ref[...])
    z = _dot(alr.astype(BF16), wa2_ref[...]) + ba_ref[...]
    log_sig = jnp.minimum(z, 0.0) - jnp.log(1.0 + jnp.exp(-jnp.abs(z)))
    la_ref[...] = log_sig * (1.0 / GATE_NORM)


def _inproj(x, l, p):
    t = x.shape[0]
    row = lambda w: pl.BlockSpec((TM, w), lambda i: (i, 0))
    lay2 = lambda w: pl.BlockSpec((None, 1, w), lambda i: (l, 0, 0))
    lay3 = lambda a, b: pl.BlockSpec((None, a, b), lambda i: (l, 0, 0), pipeline_mode=pl.Buffered(1))
    outs = [(WA, BF16), (WA, F32), (WA, F32), (WGK, F32), (WGK, F32), (WGV, F32), (WGV, F32), (WGK, F32)]
    return pl.pallas_call(
        _inproj_kernel,
        grid=(t // TM,),
        in_specs=[row(D_MODEL), lay2(D_MODEL), lay3(D_MODEL, MAIN_W), lay3(D_MODEL, LANES),
                  lay3(LANES, WGK), lay2(WGK), lay2(WA), lay2(WA),
                  pl.BlockSpec((WA, WA), lambda i: (0, 0))],
        out_specs=[row(w) for w, _ in outs],
        out_shape=[jax.ShapeDtypeStruct((t, w), dt) for w, dt in outs],
        compiler_params=_cparams(("parallel",)),
        name="inproj",
    )(x, p["norm_mix"], p["w_main"], p["w_alr"], p["w_alpha2"], p["b_alpha"], p["gq"], p["gk"], p["gsum"])


def _attn_kernel(rb_ref, lamp_ref, q_ref, k_ref, v_ref, bias_ref, o_ref,
                 kb_ref, vb_ref, qs_ref, m_ref, l_ref, acc_ref, *, lam_init):
    hd = pl.program_id(1)
    i = pl.program_id(2)

    @pl.when(i == 0)
    def _():
        kb_ref[...] = k_ref[...].astype(BF16)
        vb_ref[...] = v_ref[...].astype(BF16)

    lane = lax.broadcasted_iota(jnp.int32, (1, LANES), 1)
    q = q_ref[...]
    qs_ref[0:TQ, :] = jnp.where(lane < DHA, q, jnp.zeros_like(q))
    qs_ref[TQ:2 * TQ, :] = jnp.where(lane >= DHA, q, jnp.zeros_like(q))
    m_ref[...] = jnp.full(m_ref.shape, NEG, F32)
    l_ref[...] = jnp.zeros(l_ref.shape, F32)
    acc_ref[...] = jnp.zeros(acc_ref.shape, F32)

    def step(j, bias):
        start = pl.multiple_of(j * TK, TK)
        s = _dot_nt(qs_ref[...], kb_ref[pl.ds(start, TK), :]) + bias
        m_prev = m_ref[...]
        m_new = jnp.maximum(m_prev, jnp.max(s, axis=-1, keepdims=True))
        alpha = jnp.exp(m_prev - m_new)
        p = jnp.exp(s - m_new)
        l_ref[...] = alpha * l_ref[...] + jnp.sum(p, axis=-1, keepdims=True)
        acc_ref[...] = alpha * acc_ref[...] + _dot(p.astype(BF16), vb_ref[pl.ds(start, TK), :])
        m_ref[...] = m_new

    jl = i // 2
    odd = i % 2
    n_far = jnp.maximum(jl - 1 + odd, 0)
    far_bias = rb_ref[N_BUCKETS - 1, hd]

    def far_body(j, carry):
        step(j, far_bias)
        return carry

    lax.fori_loop(0, n_far, far_body, 0)

    @pl.when(jnp.logical_and(odd == 0, jl >= 1))
    def _():
        step(jl - 1, bias_ref[2])

    @pl.when(odd == 0)
    def _():
        step(jl, bias_ref[0])

    @pl.when(odd == 1)
    def _():
        step(jl, bias_ref[1])

    lam = _lam(lamp_ref, lam_init)
    o = acc_ref[...] / l_ref[...]
    o_ref[...] = o[0:TQ, :] - lam * o[TQ:2 * TQ, :]


def _attn_prompt(qa, ka, va, bias, rel_bias, lamp, lam_init, batch, seq):
    q3 = qa.reshape(batch, seq, WA)
    k3 = ka.reshape(batch, seq, WA)
    v3 = va.reshape(batch, seq, WA)
    kv_spec = pl.BlockSpec((None, seq, LANES), lambda b, h, i: (b, 0, h))
    blk = pl.BlockSpec((None, TQ, LANES), lambda b, h, i: (b, i, h))
    out = pl.pallas_call(
        functools.partial(_attn_kernel, lam_init=lam_init),
        grid=(batch, HA, seq // TQ),
        in_specs=[pl.BlockSpec(memory_space=pltpu.SMEM),
                  pl.BlockSpec((4, DHA), lambda b, h, i: (0, 0)),
                  blk, kv_spec, kv_spec,
                  pl.BlockSpec((None, 3, 2 * TQ, TK), lambda b, h, i: (h, 0, 0, 0))],
        out_specs=blk,
        out_shape=jax.ShapeDtypeStruct((batch, seq, WA), F32),
        scratch_shapes=[pltpu.VMEM((seq, LANES), BF16), pltpu.VMEM((seq, LANES), BF16),
                        pltpu.VMEM((2 * TQ, LANES), BF16),
                        pltpu.VMEM((2 * TQ, 1), F32), pltpu.VMEM((2 * TQ, 1), F32),
                        pltpu.VMEM((2 * TQ, LANES), F32)],
        compiler_params=_cparams(("parallel", "parallel", "arbitrary")),
        name="attn_prompt",
    )(rel_bias, lamp, q3, k3, v3, bias)
    return out.reshape(batch * seq, WA)


def _attn_sample_kernel(pt_ref, lamp_ref, q_ref, kn_ref, vn_ref, bias_ref, *rest, n_pages, ts, lam_init):
    kp = rest[:n_pages]
    vp = rest[n_pages:2 * n_pages]
    o_ref = rest[2 * n_pages]
    s_ref = rest[2 * n_pages + 1]
    nr = HA * 2 * ts
    past = n_pages * PAGE_SIZE

    q = q_ref[...].astype(F32)
    qt = jnp.concatenate([q] * (2 * HA), axis=0)
    r = lax.broadcasted_iota(jnp.int32, (nr, WA), 0)
    c = lax.broadcasted_iota(jnp.int32, (nr, WA), 1)
    keep = lax.shift_right_logical(c, int(math.log2(DHA))) == lax.shift_right_logical(r, int(math.log2(ts)))
    qbd = jnp.where(keep, qt, 0.0).astype(BF16)
    zpad = jnp.zeros((PAGE_SIZE - ts, WA), F32)
    knp = jnp.concatenate([kn_ref[...], zpad], axis=0).astype(BF16)
    vnp = jnp.concatenate([vn_ref[...], zpad], axis=0).astype(BF16)

    for pg in range(n_pages):
        s_ref[:, pg * PAGE_SIZE:(pg + 1) * PAGE_SIZE] = _dot_nt(qbd, kp[pg][...].astype(BF16))
    s_ref[:, past:past + PAGE_SIZE] = _dot_nt(qbd, knp)

    s = s_ref[...] + bias_ref[...]
    m = jnp.max(s, axis=-1, keepdims=True)
    p = jnp.exp(s - m)
    l = jnp.sum(p, axis=-1, keepdims=True)
    pb = p.astype(BF16)
    acc = _dot(pb[:, past:past + PAGE_SIZE], vnp)
    for pg in range(n_pages):
        acc = acc + _dot(pb[:, pg * PAGE_SIZE:(pg + 1) * PAGE_SIZE], vp[pg][...].astype(BF16))
    acc = acc / l
    lam = _lam(lamp_ref, lam_init)
    for h in range(HA):
        r0 = h * 2 * ts
        cs = slice(h * LANES, (h + 1) * LANES)
        o_ref[:, cs] = acc[r0:r0 + ts, cs] - lam * acc[r0 + ts:r0 + 2 * ts, cs]


def _attn_sample(qa, ka, va, cache_k, cache_v, page_table, bias, lamp, l, lam_init, dec_batch, ts):
    n_pages = page_table.shape[1]
    n_pool = cache_k.shape[1]
    ck = cache_k.reshape(DEPTH, n_pool, PAGE_SIZE, WA)
    cv = cache_v.reshape(DEPTH, n_pool, PAGE_SIZE, WA)
    nr = HA * 2 * ts
    width = n_pages * PAGE_SIZE + PAGE_SIZE
    tok = pl.BlockSpec((None, ts, WA), lambda b, pt: (b, 0, 0))
    page = lambda pg: pl.BlockSpec((None, None, PAGE_SIZE, WA), lambda b, pt: (l, pt[b, pg], 0, 0))
    grid_spec = pltpu.PrefetchScalarGridSpec(
        num_scalar_prefetch=1,
        grid=(dec_batch,),
        in_specs=[pl.BlockSpec((4, DHA), lambda b, pt: (0, 0)), tok, tok, tok,
                  pl.BlockSpec((nr, width), lambda b, pt: (0, 0))]
                 + [page(pg) for pg in range(n_pages)] * 2,
        out_specs=tok,
        scratch_shapes=[pltpu.VMEM((nr, width), F32)],
    )
    out = pl.pallas_call(
        functools.partial(_attn_sample_kernel, n_pages=n_pages, ts=ts, lam_init=lam_init),
        grid_spec=grid_spec,
        out_shape=jax.ShapeDtypeStruct((dec_batch, ts, WA), F32),
        compiler_params=_cparams(("parallel",)),
        name="attn_sample",
    )(page_table, lamp, qa.reshape(dec_batch, ts, WA), ka.reshape(dec_batch, ts, WA),
      va.reshape(dec_batch, ts, WA), bias, *([ck] * n_pages), *([cv] * n_pages))
    return out.reshape(dec_batch * ts, WA)


def _split3(x):
    x1 = x.astype(BF16)
    r1 = x - x1.astype(F32)
    x2 = r1.astype(BF16)
    x3 = (r1 - x2.astype(F32)).astype(BF16)
    return x1, x2, x3


def _gla_kernel(q_ref, k_ref, v_ref, g_ref, s0_ref, tril_ref, o_ref, sout_ref, s_scr, *, nb, c_in, c):
    ci = pl.program_id(1)

    @pl.when(ci == 0)
    def _():
        s_scr[...] = s0_ref[...]

    def pad(x):
        if c_in == c:
            return x
        return jnp.concatenate([x, jnp.zeros((c - c_in, x.shape[1]), x.dtype)], axis=0)

    lane = lax.broadcasted_iota(jnp.int32, (1, LANES), 1)
    head0 = lane < DKG
    row_head0 = lax.broadcasted_iota(jnp.int32, (LANES, LANES), 0) < DKG
    tt = lax.broadcasted_iota(jnp.int32, (c, c), 0)
    ss = lax.broadcasted_iota(jnp.int32, (c, c), 1)
    causal = tt >= ss
    tril = tril_ref[...]
    ones = jnp.ones((c, LANES), BF16)
    mid = c // 2 - 1

    for b in range(nb):
        for pr in range(HG // 2):
            ks = slice(pr * LANES, (pr + 1) * LANES)
            q = pad(q_ref[b, :, ks])
            k = pad(k_ref[b, :, ks])
            g = pad(g_ref[b, :, ks])
            g3 = _split3(g)
            bcum = _dot(tril, g3[0]) + _dot(tril, g3[1]) + _dot(tril, g3[2])
            tot = _dot_tn(g3[0], ones) + _dot_tn(g3[1], ones) + _dot_tn(g3[2], ones)
            bm = bcum[mid:mid + 1, :]
            bl = bcum[c - 1:c, :]
            qi = q * jnp.exp(bcum)
            qt = q * jnp.exp(bcum - bm)
            kt = (k * jnp.exp(bm - bcum)).astype(BF16)
            kd = (k * jnp.exp(bl - bcum)).astype(BF16)
            s_old = s_scr[b, pr]
            sb = s_old.astype(BF16)
            upd = []
            for hh in range(2):
                sel = head0 if hh == 0 else jnp.logical_not(head0)
                vcol = slice((2 * pr + hh) * DVG, (2 * pr + hh + 1) * DVG)
                vh = pad(v_ref[b, :, vcol]).astype(BF16)
                a = _dot_nt(jnp.where(sel, qt, 0.0).astype(BF16), kt)
                a = jnp.where(causal, a, 0.0)
                o = _dot(a.astype(BF16), vh) + _dot(jnp.where(sel, qi, 0.0).astype(BF16), sb)
                o_ref[b, :, vcol] = o[0:c_in, :]
                upd.append(_dot_tn(kd, vh))
            s_scr[b, pr] = jnp.exp(tot) * s_old + jnp.where(row_head0, upd[0], upd[1])

    @pl.when(ci == pl.num_programs(1) - 1)
    def _():
        sout_ref[...] = s_scr[...]


def _gla(qg, kg, vg, la, s0, batch, seq, nb, c_in, c):
    n_chunks = seq // c_in
    tril = jnp.tril(jnp.ones((c, c), BF16))
    tok = lambda w: pl.BlockSpec((nb, c_in, w), lambda bi, ci: (bi, ci, 0))
    st = pl.BlockSpec((nb, HG // 2, 2 * DKG, DVG), lambda bi, ci: (bi, 0, 0, 0))
    o, s_new = pl.pallas_call(
        functools.partial(_gla_kernel, nb=nb, c_in=c_in, c=c),
        grid=(batch // nb, n_chunks),
        in_specs=[tok(WGK), tok(WGK), tok(WGV), tok(WGK), st,
                  pl.BlockSpec((c, c), lambda bi, ci: (0, 0))],
        out_specs=[tok(WGV), st],
        out_shape=[jax.ShapeDtypeStruct((batch, seq, WGV), F32),
                   jax.ShapeDtypeStruct((batch, HG // 2, 2 * DKG, DVG), F32)],
        scratch_shapes=[pltpu.VMEM((nb, HG // 2, 2 * DKG, DVG), F32)],
        compiler_params=_cparams(("parallel", "arbitrary")),
        name="gla",
    )(qg.reshape(batch, seq, WGK), kg.reshape(batch, seq, WGK), vg.reshape(batch, seq, WGV),
      la.reshape(batch, seq, WGK), s0.reshape(batch, HG // 2, 2 * DKG, DVG), tril)
    return o.reshape(batch * seq, WGV), s_new.reshape(batch, HG, DKG, DVG)


def _merge_kernel(x_ref, oa_ref, og_ref, rg_ref, gmix_ref, subln_ref, glan_ref,
                  wgate_ref, wdo_ref, wgo_ref, wout_ref, o_ref, ya_ref, yg_ref, *, lam_init):
    x = x_ref[...]
    h = _rms(x, gmix_ref[...]).astype(BF16)
    gate = jax.nn.sigmoid(_dot(h, wgate_ref[...]))
    for hd in range(HA):
        cs = slice(hd * LANES, (hd + 1) * LANES)
        ya_ref[:, cs] = (_rms(oa_ref[:, cs], subln_ref[...]) * (1.0 - lam_init)).astype(BF16)
        yg_ref[:, cs] = (_rms(og_ref[:, cs], glan_ref[...]) * jax.nn.silu(rg_ref[:, cs])).astype(BF16)
    ya = _dot(ya_ref[...], wdo_ref[...])
    yg = _dot(yg_ref[...], wgo_ref[...])
    mix = gate[:, 0:D_MODEL] * ya + gate[:, D_MODEL:2 * D_MODEL] * yg
    o_ref[...] = x + _dot(mix.astype(BF16), wout_ref[...])


def _merge(x, oa, og, rg, l, p, lam_init):
    t = x.shape[0]
    row = lambda w: pl.BlockSpec((TM, w), lambda i: (i, 0))
    lay2 = lambda w: pl.BlockSpec((None, 1, w), lambda i: (l, 0, 0))
    lay3 = lambda a, b: pl.BlockSpec((None, a, b), lambda i: (l, 0, 0), pipeline_mode=pl.Buffered(1))
    return pl.pallas_call(
        functools.partial(_merge_kernel, lam_init=lam_init),
        grid=(t // TM,),
        in_specs=[row(D_MODEL), row(WA), row(WGV), row(WGV), lay2(D_MODEL), lay2(2 * DHA), lay2(DVG),
                  lay3(D_MODEL, 2 * D_MODEL), lay3(WA, D_MODEL), lay3(WGV, D_MODEL), lay3(D_MODEL, D_MODEL)],
        out_specs=row(D_MODEL),
        out_shape=jax.ShapeDtypeStruct((t, D_MODEL), F32),
        scratch_shapes=[pltpu.VMEM((TM, WA), BF16), pltpu.VMEM((TM, WGV), BF16)],
        compiler_params=_cparams(("parallel",)),
        name="merge",
    )(x, oa, og, rg, p["norm_mix"], p["subln"], p["gla_norm"],
      p["w_gate"], p["w_diff_out"], p["w_gla_out"], p["w_out"])


def _ffn_kernel(x_ref, g_ref, wg_ref, wu_ref, wd_ref, o_ref, *, chunk):
    x = x_ref[...]
    h = _rms(x, g_ref[...]).astype(BF16)
    acc = x
    for c0 in range(0, wg_ref.shape[1], chunk):
        a = jax.nn.silu(_dot(h, wg_ref[:, c0:c0 + chunk])) * _dot(h, wu_ref[:, c0:c0 + chunk])
        acc = acc + _dot(a.astype(BF16), wd_ref[c0:c0 + chunk, :])
    o_ref[...] = acc


def _ffn(x, l, p):
    t = x.shape[0]
    f = p["w_ffn_gate"].shape[2]
    row = pl.BlockSpec((TM, D_MODEL), lambda i: (i, 0))
    once = pl.Buffered(1)
    return pl.pallas_call(
        functools.partial(_ffn_kernel, chunk=f // 2),
        grid=(t // TM,),
        in_specs=[row, pl.BlockSpec((None, 1, D_MODEL), lambda i: (l, 0, 0)),
                  pl.BlockSpec((None, D_MODEL, f), lambda i: (l, 0, 0), pipeline_mode=once),
                  pl.BlockSpec((None, D_MODEL, f), lambda i: (l, 0, 0), pipeline_mode=once),
                  pl.BlockSpec((None, f, D_MODEL), lambda i: (l, 0, 0), pipeline_mode=once)],
        out_specs=row,
        out_shape=jax.ShapeDtypeStruct((t, D_MODEL), F32),
        compiler_params=_cparams(("parallel",)),
        name="ffn",
    )(x, p["norm_ffn"], p["w_ffn_gate"], p["w_ffn_up"], p["w_ffn_down"])


def kernel(x_prompt, x_sample, cache_k, cache_v, state_gla, page_table, rel_bias, norm_mix, w_in, w_alpha2, b_alpha, qn_gain, kn_gain, lam_q1, lam_k1, lam_q2, lam_k2, subln_gain, gla_norm_gain, w_diff_out, w_gla_out, w_out, norm_ffn, w_ffn_gate, w_ffn_up, w_ffn_down):
    batch, seq, _ = x_prompt.shape
    dec_batch, ts, _ = x_sample.shape
    n_pages = page_table.shape[1]
    past = n_pages * PAGE_SIZE
    depth = w_in.shape[0]
    gate0 = MAIN_W + GATE_RANK

    vec = lambda a: a.reshape(depth, 1, a.shape[-1])
    p = {
        "norm_mix": vec(norm_mix), "norm_ffn": vec(norm_ffn), "b_alpha": vec(b_alpha),
        "gq": vec(jnp.tile(qn_gain, (1, WA // DHA))), "gk": vec(jnp.tile(kn_gain, (1, WA // DHA))),
        "subln": vec(subln_gain), "gla_norm": vec(gla_norm_gain),
        "w_main": w_in[:, :, :MAIN_W].astype(BF16),
        "w_alr": jnp.pad(w_in[:, :, MAIN_W:gate0], ((0, 0), (0, 0), (0, LANES - GATE_RANK))).astype(BF16),
        "w_gate": w_in[:, :, gate0:].astype(BF16),
        "w_alpha2": jnp.pad(w_alpha2, ((0, 0), (0, LANES - GATE_RANK), (0, 0))).astype(BF16),
        "w_diff_out": w_diff_out.astype(BF16), "w_gla_out": w_gla_out.astype(BF16), "w_out": w_out.astype(BF16),
        "w_ffn_gate": w_ffn_gate.astype(BF16), "w_ffn_up": w_ffn_up.astype(BF16),
        "w_ffn_down": w_ffn_down.astype(BF16),
        "gsum": jnp.kron(jnp.eye(WA // DHA, dtype=F32), jnp.full((DHA, DHA), 1.0 / DHA, F32)).astype(BF16),
    }
    lamp = jnp.stack([lam_q1, lam_k1, lam_q2, lam_k2], axis=1)

    bias_p = _bias_tables(rel_bias, 3, 2 * TQ, TK, TQ, 0, TQ)
    bias_s = _bias_tables(rel_bias, 1, 2 * ts, past + PAGE_SIZE, ts, past, 0)
    bias_s = bias_s.reshape(HA * 2 * ts, past + PAGE_SIZE)

    xp = x_prompt.reshape(batch * seq, D_MODEL)
    xs = x_sample.reshape(dec_batch * ts, D_MODEL)
    s0p = jnp.zeros((batch, HG, DKG, DVG), F32)
    kp_l, vp_l, sp_l, ks_l, vs_l, ss_l = [], [], [], [], [], []
    for l in range(depth):
        lam_init = 0.8 - 0.6 * math.exp(-0.3 * l)
        qa, ka, va, qg, kg, vg, rg, la = _inproj(xp, l, p)
        oa = _attn_prompt(qa, ka, va, bias_p, rel_bias, lamp[l], lam_init, batch, seq)
        og, sp = _gla(qg, kg, vg, la, s0p, batch, seq, batch, GLA_C, GLA_C)
        xp = _merge(xp, oa, og, rg, l, p, lam_init)
        xp = _ffn(xp, l, p)
        kp_l.append(ka)
        vp_l.append(va)
        sp_l.append(sp)
        qa, ka, va, qg, kg, vg, rg, la = _inproj(xs, l, p)
        oa = _attn_sample(qa, ka, va, cache_k, cache_v, page_table, bias_s, lamp[l], l, lam_init, dec_batch, ts)
        og, ss = _gla(qg, kg, vg, la, state_gla[l], dec_batch, ts, GLA_SB, ts, 2 * ts)
        xs = _merge(xs, oa, og, rg, l, p, lam_init)
        xs = _ffn(xs, l, p)
        ks_l.append(ka)
        vs_l.append(va)
        ss_l.append(ss)
    return (xp.reshape(batch, seq, D_MODEL), xs.reshape(dec_batch, ts, D_MODEL),
            jnp.stack(kp_l).reshape(depth, batch, seq, HA, 2, DHA),
            jnp.stack(vp_l).reshape(depth, batch, seq, HA, 2 * DHA),
            jnp.stack(sp_l),
            jnp.stack(ks_l).reshape(depth, dec_batch, ts, HA, 2, DHA),
            jnp.stack(vs_l).reshape(depth, dec_batch, ts, HA, 2 * DHA),
            jnp.stack(ss_l))
```

```python
import functools
import math

import jax
import jax.numpy as jnp
from jax import lax
from jax.experimental import pallas as pl
from jax.experimental.pallas import tpu as pltpu

F32 = jnp.float32
BF16 = jnp.bfloat16

D_MODEL = 1024
HA = 4
DHA = 64
HG = 4
DKG = 64
DVG = 128
GATE_RANK = 16
GATE_NORM = 16.0
N_BUCKETS = 32
MAX_DISTANCE = 128
PAGE_SIZE = 128
EPS = 1e-6
WA = HA * 2 * DHA
WGK = HG * DKG
WGV = HG * DVG
MAIN_W = 3 * WA + 2 * WGK + 2 * WGV
LANES = 128
NEG = -1e30
LOG2E = math.log2(math.e)

TM = 512
TQ = 256
TK = 512
GLA_C = 64
GLA_SB = 8
VMEM_LIMIT = 56 * 1024 * 1024


def _cparams(sem):
    return pltpu.CompilerParams(dimension_semantics=sem, vmem_limit_bytes=VMEM_LIMIT)


def _rms(x, gain):
    ms = jnp.mean(x * x, axis=-1, keepdims=True)
    return x * lax.rsqrt(ms + EPS) * gain


def _dot(a, b):
    return jnp.dot(a, b, preferred_element_type=F32)


def _dot_nt(a, b):
    return lax.dot_general(a, b, (((1,), (1,)), ((), ())), preferred_element_type=F32)


def _dot_tn(a, b):
    return lax.dot_general(a, b, (((0,), (0,)), ((), ())), preferred_element_type=F32)


def _lam(lamp_ref, lam_init):
    a = jnp.sum(lamp_ref[0:1, :] * lamp_ref[1:2, :], axis=-1, keepdims=True)
    b = jnp.sum(lamp_ref[2:3, :] * lamp_ref[3:4, :], axis=-1, keepdims=True)
    return jnp.exp(a) - jnp.exp(b) + lam_init


def _bias_kernel(rb_ref, o_ref, *, period, off0, off_step, transposed, shifted):
    h = pl.program_id(0)
    t = pl.program_id(1)
    rows, cols = o_ref.shape
    r = lax.broadcasted_iota(jnp.int32, (rows, cols), 0)
    c = lax.broadcasted_iota(jnp.int32, (rows, cols), 1)
    if transposed:
        r, c = c, r
    d = off0 + t * off_step + jnp.bitwise_and(r, period - 1) - c
    n = jnp.maximum(d, 0)
    max_exact = N_BUCKETS // 2
    nf = jnp.maximum(n, 1).astype(F32)
    large = max_exact + (jnp.log(nf / max_exact) / math.log(MAX_DISTANCE / max_exact)
                         * (N_BUCKETS - max_exact)).astype(jnp.int32)
    large = jnp.minimum(large, N_BUCKETS - 1)
    bucket = jnp.where(n < max_exact, n, large)
    val = jnp.zeros((rows, cols), F32)
    for k in range(N_BUCKETS):
        val = jnp.where(bucket == k, rb_ref[k, h], val)
    if shifted:
        val = val - rb_ref[N_BUCKETS - 1, h]
    o_ref[...] = jnp.where(d >= 0, val * LOG2E, NEG)


def _bias_tables(rel_bias, n_t, rows, cols, period, off0, off_step, transposed, shifted):
    return pl.pallas_call(
        functools.partial(_bias_kernel, period=period, off0=off0, off_step=off_step,
                          transposed=transposed, shifted=shifted),
        grid=(HA, n_t),
        in_specs=[pl.BlockSpec(memory_space=pltpu.SMEM)],
        out_specs=pl.BlockSpec((None, None, rows, cols), lambda h, t: (h, t, 0, 0)),
        out_shape=jax.ShapeDtypeStruct((HA, n_t, rows, cols), F32),
        compiler_params=_cparams(("arbitrary", "arbitrary")),
        name="bias_tables",
    )(rel_bias)


def _inproj_kernel(*refs, prompt, n_alias):
    (x_ref, gmix_ref, w_ref, walr_ref, wa2_ref, ba_ref, gq_ref, gk_ref, gsum_ref) = refs[:9]
    (qa_ref, ka_ref, va_ref, qg_ref, kg_ref, vg_ref, rg_ref, la_ref) = refs[9 + n_alias:]
    h = _rms(x_ref[...], gmix_ref[...]).astype(BF16)
    proj = _dot(h, w_ref[...])
    gs = gsum_ref[...]
    qa = proj[:, 0:WA]
    ka = proj[:, WA:2 * WA]
    qms = _dot((qa * qa).astype(BF16), gs)
    kms = _dot((ka * ka).astype(BF16), gs)
    qa_ref[...] = (qa * lax.rsqrt(qms + EPS) * gq_ref[...] * (DHA ** -0.5 * LOG2E)).astype(BF16)
    kan = ka * lax.rsqrt(kms + EPS) * gk_ref[...]
    if prompt:
        ka_ref[...] = kan.T
        for hd in range(HA):
            va_ref[pl.ds(hd, TM, stride=HA), :] = proj[:, 2 * WA + hd * LANES:2 * WA + (hd + 1) * LANES]
    else:
        ka_ref[...] = kan
        va_ref[...] = proj[:, 2 * WA:3 * WA]
    o = 3 * WA
    qg_ref[...] = proj[:, o:o + WGK] * (DKG ** -0.5)
    kg_ref[...] = proj[:, o + WGK:o + 2 * WGK]
    o += 2 * WGK
    vg_ref[...] = proj[:, o:o + WGV]
    rg_ref[...] = proj[:, o + WGV:o + 2 * WGV]
    alr = _dot(h, walr_ref[...])
    z = _dot(alr.astype(BF16), wa2_ref[...]) + ba_ref[...]
    log_sig = jnp.minimum(z, 0.0) - jnp.log(1.0 + jnp.exp(-jnp.abs(z)))
    la_ref[...] = log_sig * (1.0 / GATE_NORM)


def _inproj(x, l, p, kv_prev=None, batch=None, seq=None):
    t = x.shape[0]
    depth = p["w_main"].shape[0]
    prompt = kv_prev is not None
    row = lambda w: pl.BlockSpec((TM, w), lambda i: (i, 0))
    lay2 = lambda w: pl.BlockSpec((None, 1, w), lambda i: (l, 0, 0))
    lay3 = lambda a, b: pl.BlockSpec((None, a, b), lambda i: (l, 0, 0), pipeline_mode=pl.Buffered(1))
    in_specs = [row(D_MODEL), lay2(D_MODEL), lay3(D_MODEL, MAIN_W), lay3(D_MODEL, LANES),
                lay3(LANES, WGK), lay2(WGK), lay2(WA), lay2(WA),
                pl.BlockSpec((WA, WA), lambda i: (0, 0))]
    args = [x, p["norm_mix"], p["w_main"], p["w_alr"], p["w_alpha2"], p["b_alpha"], p["gq"], p["gk"], p["gsum"]]
    rest = [(WGK, F32), (WGK, F32), (WGV, F32), (WGV, F32), (WGK, F32)]
    if prompt:
        spb = seq // TM
        kv_specs = [pl.BlockSpec((None, None, WA, TM), lambda i: (l, i // spb, 0, i % spb)),
                    pl.BlockSpec((None, TM * HA, LANES), lambda i: (l, i, 0))]
        kv_shapes = [jax.ShapeDtypeStruct((depth, batch, WA, seq), F32),
                     jax.ShapeDtypeStruct((depth, t * HA, LANES), F32)]
        aliases = {}
        for n, a in enumerate(kv_prev):
            in_specs.append(pl.BlockSpec(memory_space=pl.ANY))
            args.append(a)
            aliases[9 + n] = 1 + n
    else:
        kv_specs = [row(WA), row(WA)]
        kv_shapes = [jax.ShapeDtypeStruct((t, WA), F32)] * 2
        aliases = {}
    return pl.pallas_call(
        functools.partial(_inproj_kernel, prompt=prompt, n_alias=len(aliases)),
        grid=(t // TM,),
        in_specs=in_specs,
        out_specs=[row(WA)] + kv_specs + [row(w) for w, _ in rest],
        out_shape=[jax.ShapeDtypeStruct((t, WA), BF16)] + kv_shapes
                  + [jax.ShapeDtypeStruct((t, w), dt) for w, dt in rest],
        input_output_aliases=aliases,
        compiler_params=_cparams(("parallel",)),
        name="inproj",
    )(*args)


def _attn_kernel(lamp_ref, q_ref, kt_ref, v_ref, bias_ref, o_ref,
                 kb_ref, vtb_ref, qs_ref, m_ref, l_ref, acc_ref, sa_ref, sb_ref, *, lam_init, seq):
    hd = pl.program_id(1)
    i = pl.program_id(2)

    @pl.when(i == 0)
    def _():
        for c0 in range(0, seq, TK):
            kb_ref[c0:c0 + TK, :] = kt_ref[:, c0:c0 + TK].T.astype(BF16)
            vh = v_ref[pl.ds(c0 * HA + hd, TK, stride=HA), :]
            vtb_ref[:, c0:c0 + TK] = vh.T.astype(BF16)

    lane = lax.broadcasted_iota(jnp.int32, (1, LANES), 1)
    q = q_ref[...]
    qs_ref[0:TQ, :] = jnp.where(lane < DHA, q, jnp.zeros_like(q))
    qs_ref[TQ:2 * TQ, :] = jnp.where(lane >= DHA, q, jnp.zeros_like(q))
    m_ref[...] = jnp.full(m_ref.shape, NEG, F32)
    l_ref[...] = jnp.zeros(l_ref.shape, F32)
    acc_ref[...] = jnp.zeros(acc_ref.shape, F32)

    jl = i // 2
    odd = i % 2

    def logits(j, dst_ref):
        start = pl.multiple_of(jnp.minimum(j, jl) * TK, TK)
        dst_ref[...] = _dot_nt(kb_ref[pl.ds(start, TK), :], qs_ref[...])

    def update(src_ref, j):
        near = jnp.logical_and(j == jl - 1, odd == 0)
        table = jnp.where(j == jl, odd, jnp.where(near, 2, 3))
        start = pl.multiple_of(j * TK, TK)
        s = src_ref[...] + bias_ref[table]
        m_prev = m_ref[...]
        m_new = jnp.maximum(m_prev, jnp.max(s, axis=0, keepdims=True))
        alpha = jnp.exp2(m_prev - m_new)
        p = jnp.exp2(s - m_new)
        l_ref[...] = alpha * l_ref[...] + jnp.sum(p, axis=0, keepdims=True)
        acc_ref[...] = alpha * acc_ref[...] + _dot(vtb_ref[:, pl.ds(start, TK)], p.astype(BF16))
        m_ref[...] = m_new

    logits(0, sa_ref)

    def pair_body(jj, carry):
        j0 = 2 * jj
        logits(j0 + 1, sb_ref)
        update(sa_ref, j0)

        @pl.when(j0 + 1 <= jl)
        def _():
            logits(j0 + 2, sa_ref)
            update(sb_ref, j0 + 1)

        return carry

    lax.fori_loop(0, jl // 2 + 1, pair_body, 0)

    lam = _lam(lamp_ref, lam_init)
    o = acc_ref[...] / l_ref[...]
    o_ref[...] = (o[:, 0:TQ] - lam * o[:, TQ:2 * TQ]).T


def _attn_prompt(qa, kt_all, v4_all, bias, lamp, l, lam_init, batch, seq):
    depth = kt_all.shape[0]
    q3 = qa.reshape(batch, seq, WA)
    v3 = v4_all.reshape(depth * batch, seq * HA, LANES)
    blk = pl.BlockSpec((None, TQ, LANES), lambda b, h, i: (b, i, h))
    out = pl.pallas_call(
        functools.partial(_attn_kernel, lam_init=lam_init, seq=seq),
        grid=(batch, HA, seq // TQ),
        in_specs=[pl.BlockSpec((4, DHA), lambda b, h, i: (0, 0)),
                  blk,
                  pl.BlockSpec((None, None, LANES, seq), lambda b, h, i: (l, b, h, 0)),
                  pl.BlockSpec((None, seq * HA, LANES), lambda b, h, i: (l * batch + b, 0, 0)),
                  pl.BlockSpec((None, 4, TK, 2 * TQ), lambda b, h, i: (h, 0, 0, 0))],
        out_specs=blk,
        out_shape=jax.ShapeDtypeStruct((batch, seq, WA), F32),
        scratch_shapes=[pltpu.VMEM((seq, LANES), BF16), pltpu.VMEM((LANES, seq), BF16),
                        pltpu.VMEM((2 * TQ, LANES), BF16),
                        pltpu.VMEM((1, 2 * TQ), F32), pltpu.VMEM((1, 2 * TQ), F32),
                        pltpu.VMEM((LANES, 2 * TQ), F32),
                        pltpu.VMEM((TK, 2 * TQ), F32), pltpu.VMEM((TK, 2 * TQ), F32)],
        compiler_params=_cparams(("parallel", "parallel", "arbitrary")),
        name="attn_prompt",
    )(lamp, q3, kt_all, v3, bias)
    return out.reshape(batch * seq, WA)


def _attn_sample_kernel(pt_ref, lamp_ref, q_ref, kn_ref, vn_ref, bias_ref, *rest, n_pages, ts, lam_init):
    kp = rest[:n_pages]
    vp = rest[n_pages:2 * n_pages]
    o_ref = rest[2 * n_pages]
    s_ref = rest[2 * n_pages + 1]
    nr = HA * 2 * ts
    past = n_pages * PAGE_SIZE

    q = q_ref[...].astype(F32)
    qt = jnp.concatenate([q] * (2 * HA), axis=0)
    r = lax.broadcasted_iota(jnp.int32, (nr, WA), 0)
    c = lax.broadcasted_iota(jnp.int32, (nr, WA), 1)
    keep = lax.shift_right_logical(c, int(math.log2(DHA))) == lax.shift_right_logical(r, int(math.log2(ts)))
    qbd = jnp.where(keep, qt, 0.0).astype(BF16)
    zpad = jnp.zeros((PAGE_SIZE - ts, WA), F32)
    knp = jnp.concatenate([kn_ref[...], zpad], axis=0).astype(BF16)
    vnp = jnp.concatenate([vn_ref[...], zpad], axis=0).astype(BF16)

    for pg in range(n_pages):
        s_ref[:, pg * PAGE_SIZE:(pg + 1) * PAGE_SIZE] = _dot(qbd, kp[pg][...].astype(BF16))
    s_ref[:, past:past + PAGE_SIZE] = _dot_nt(qbd, knp)

    s = s_ref[...] + bias_ref[...]
    m = jnp.max(s, axis=-1, keepdims=True)
    p = jnp.exp2(s - m)
    l = jnp.sum(p, axis=-1, keepdims=True)
    pb = p.astype(BF16)
    lam = _lam(lamp_ref, lam_init)
    for h in range(HA):
        rs = slice(h * 2 * ts, (h + 1) * 2 * ts)
        cs = slice(h * LANES, (h + 1) * LANES)
        acc = _dot(pb[rs, past:past + PAGE_SIZE], vnp[:, cs])
        for pg in range(n_pages):
            vh = vp[pg][pl.ds(h, PAGE_SIZE, stride=HA), :].astype(BF16)
            acc = acc + _dot(pb[rs, pg * PAGE_SIZE:(pg + 1) * PAGE_SIZE], vh)
        acc = acc / l[rs]
        o_ref[:, cs] = acc[0:ts, :] - lam * acc[ts:2 * ts, :]


def _attn_sample(qa, ka, va, ckt, cv4, page_table, bias, lamp, l, lam_init, dec_batch, ts):
    n_pages = page_table.shape[1]
    nr = HA * 2 * ts
    width = n_pages * PAGE_SIZE + PAGE_SIZE
    tok = pl.BlockSpec((None, ts, WA), lambda b, pt: (b, 0, 0))
    page = lambda pg: pl.BlockSpec((None, None, HA * PAGE_SIZE, LANES), lambda b, pt: (l, pt[b, pg], 0, 0))
    grid_spec = pltpu.PrefetchScalarGridSpec(
        num_scalar_prefetch=1,
        grid=(dec_batch,),
        in_specs=[pl.BlockSpec((4, DHA), lambda b, pt: (0, 0)), tok, tok, tok,
                  pl.BlockSpec((nr, width), lambda b, pt: (0, 0))]
                 + [page(pg) for pg in range(n_pages)] * 2,
        out_specs=tok,
        scratch_shapes=[pltpu.VMEM((nr, width), F32)],
    )
    out = pl.pallas_call(
        functools.partial(_attn_sample_kernel, n_pages=n_pages, ts=ts, lam_init=lam_init),
        grid_spec=grid_spec,
        out_shape=jax.ShapeDtypeStruct((dec_batch, ts, WA), F32),
        compiler_params=_cparams(("parallel",)),
        name="attn_sample",
    )(page_table, lamp, qa.reshape(dec_batch, ts, WA), ka.reshape(dec_batch, ts, WA),
      va.reshape(dec_batch, ts, WA), bias, *([ckt] * n_pages), *([cv4] * n_pages))
    return out.reshape(dec_batch * ts, WA)


def _split3(x):
    x1 = x.astype(BF16)
    r1 = x - x1.astype(F32)
    x2 = r1.astype(BF16)
    x3 = (r1 - x2.astype(F32)).astype(BF16)
    return x1, x2, x3


def _gla_kernel(q_ref, k_ref, v_ref, g_ref, s0_ref, tril_ref, o_ref, sout_ref, st_scr, *, nb, c_in, c):
    ci = pl.program_id(1)
    units = [(b, pr) for b in range(nb) for pr in range(HG // 2)]

    @pl.when(ci == 0)
    def _():
        for b, pr in units:
            st_scr[b, pr] = s0_ref[b, pr].T

    def pad(x):
        if c_in == c:
            return x
        return jnp.concatenate([x, jnp.zeros((c - c_in, x.shape[1]), x.dtype)], axis=0)

    lane = lax.broadcasted_iota(jnp.int32, (1, LANES), 1)
    head0 = lane < DKG
    tt = lax.broadcasted_iota(jnp.int32, (2 * c, c), 0)
    ss = lax.broadcasted_iota(jnp.int32, (2 * c, c), 1)
    causal = jnp.bitwise_and(tt, c - 1) >= ss
    mid = c // 2 - 1

    def heads_on_rows(x):
        return jnp.concatenate([jnp.where(head0, x, 0.0), jnp.where(head0, 0.0, x)], axis=0).astype(BF16)

    g_all = jnp.concatenate([pad(g_ref[b]) for b in range(nb)], axis=1)
    tril = tril_ref[...]
    g3 = _split3(g_all)
    bc_all = _dot(tril, g3[0]) + _dot(tril, g3[1]) + _dot(tril, g3[2])

    prep = []
    for b, pr in units:
        ks = slice(pr * LANES, (pr + 1) * LANES)
        q = pad(q_ref[b, :, ks])
        k = pad(k_ref[b, :, ks])
        bcum = bc_all[:, b * WGK + pr * LANES:b * WGK + (pr + 1) * LANES]
        bm = bcum[mid:mid + 1, :]
        bl = bcum[c - 1:c, :]
        prep.append(dict(
            qt=heads_on_rows(q * jnp.exp(bcum - bm)),
            qi=heads_on_rows(q * jnp.exp(bcum)),
            kt=(k * jnp.exp(bm - bcum)).astype(BF16),
            kd=(k * jnp.exp(bl - bcum)).astype(BF16),
            decay=jnp.exp(bl),
            v2=pad(v_ref[b, :, 2 * pr * DVG:2 * (pr + 1) * DVG]).astype(BF16)))

    amat = [jnp.where(causal, _dot_nt(u["qt"], u["kt"]), 0.0).astype(BF16) for u in prep]
    inter = [_dot_nt(u["qi"], st_scr[b, pr].astype(BF16)) for u, (b, pr) in zip(prep, units)]
    for u, a, it, (b, pr) in zip(prep, amat, inter, units):
        for hh in range(2):
            o = _dot(a[hh * c:(hh + 1) * c, :], u["v2"][:, hh * DVG:(hh + 1) * DVG]) + it[hh * c:(hh + 1) * c, :]
            o_ref[b, :, (2 * pr + hh) * DVG:(2 * pr + hh + 1) * DVG] = o[0:c_in, :]
    for u, (b, pr) in zip(prep, units):
        upd = _dot_tn(u["v2"], u["kd"])
        st_scr[b, pr] = st_scr[b, pr] * u["decay"] + jnp.where(head0, upd[0:DVG, :], upd[DVG:2 * DVG, :])

    @pl.when(ci == pl.num_programs(1) - 1)
    def _():
        for b, pr in units:
            sout_ref[b, pr] = st_scr[b, pr].T


def _gla(qg, kg, vg, la, s0, s0_layer, batch, seq, nb, c_in, c):
    n_chunks = seq // c_in
    tril = jnp.tril(jnp.ones((c, c), BF16))
    tok = lambda w: pl.BlockSpec((nb, c_in, w), lambda bi, ci: (bi, ci, 0))
    st = pl.BlockSpec((nb, HG // 2, 2 * DKG, DVG), lambda bi, ci: (bi, 0, 0, 0))
    st_in = pl.BlockSpec((None, nb, HG // 2, 2 * DKG, DVG), lambda bi, ci: (s0_layer, bi, 0, 0, 0))
    o, s_new = pl.pallas_call(
        functools.partial(_gla_kernel, nb=nb, c_in=c_in, c=c),
        grid=(batch // nb, n_chunks),
        in_specs=[tok(WGK), tok(WGK), tok(WGV), tok(WGK), st_in,
                  pl.BlockSpec((c, c), lambda bi, ci: (0, 0))],
        out_specs=[tok(WGV), st],
        out_shape=[jax.ShapeDtypeStruct((batch, seq, WGV), F32),
                   jax.ShapeDtypeStruct((batch, HG // 2, 2 * DKG, DVG), F32)],
        scratch_shapes=[pltpu.VMEM((nb, HG // 2, 2 * DKG, DVG), F32)],
        compiler_params=_cparams(("parallel", "arbitrary")),
        name="gla",
    )(qg.reshape(batch, seq, WGK), kg.reshape(batch, seq, WGK), vg.reshape(batch, seq, WGV),
      la.reshape(batch, seq, WGK), s0, tril)
    return o.reshape(batch * seq, WGV), s_new.reshape(batch, HG, DKG, DVG)


def _merge_kernel(x_ref, oa_ref, og_ref, rg_ref, gmix_ref, subln_ref, glan_ref,
                  wgate_ref, wdo_ref, wgo_ref, wout_ref, o_ref, ya_ref, yg_ref, *, lam_init):
    x = x_ref[...]
    h = _rms(x, gmix_ref[...]).astype(BF16)
    gate = jax.nn.sigmoid(_dot(h, wgate_ref[...]))
    for hd in range(HA):
        cs = slice(hd * LANES, (hd + 1) * LANES)
        ya_ref[:, cs] = (_rms(oa_ref[:, cs], subln_ref[...]) * (1.0 - lam_init)).astype(BF16)
        yg_ref[:, cs] = (_rms(og_ref[:, cs], glan_ref[...]) * jax.nn.silu(rg_ref[:, cs])).astype(BF16)
    ya = _dot(ya_ref[...], wdo_ref[...])
    yg = _dot(yg_ref[...], wgo_ref[...])
    mix = gate[:, 0:D_MODEL] * ya + gate[:, D_MODEL:2 * D_MODEL] * yg
    o_ref[...] = x + _dot(mix.astype(BF16), wout_ref[...])


def _merge(x, oa, og, rg, l, p, lam_init):
    t = x.shape[0]
    row = lambda w: pl.BlockSpec((TM, w), lambda i: (i, 0))
    lay2 = lambda w: pl.BlockSpec((None, 1, w), lambda i: (l, 0, 0))
    lay3 = lambda a, b: pl.BlockSpec((None, a, b), lambda i: (l, 0, 0), pipeline_mode=pl.Buffered(1))
    return pl.pallas_call(
        functools.partial(_merge_kernel, lam_init=lam_init),
        grid=(t // TM,),
        in_specs=[row(D_MODEL), row(WA), row(WGV), row(WGV), lay2(D_MODEL), lay2(2 * DHA), lay2(DVG),
                  lay3(D_MODEL, 2 * D_MODEL), lay3(WA, D_MODEL), lay3(WGV, D_MODEL), lay3(D_MODEL, D_MODEL)],
        out_specs=row(D_MODEL),
        out_shape=jax.ShapeDtypeStruct((t, D_MODEL), F32),
        scratch_shapes=[pltpu.VMEM((TM, WA), BF16), pltpu.VMEM((TM, WGV), BF16)],
        compiler_params=_cparams(("parallel",)),
        name="merge",
    )(x, oa, og, rg, p["norm_mix"], p["subln"], p["gla_norm"],
      p["w_gate"], p["w_diff_out"], p["w_gla_out"], p["w_out"])


def _ffn_kernel(x_ref, g_ref, wg_ref, wu_ref, wd_ref, o_ref, *, chunk):
    x = x_ref[...]
    h = _rms(x, g_ref[...]).astype(BF16)
    acc = x
    for c0 in range(0, wg_ref.shape[1], chunk):
        a = jax.nn.silu(_dot(h, wg_ref[:, c0:c0 + chunk])) * _dot(h, wu_ref[:, c0:c0 + chunk])
        acc = acc + _dot(a.astype(BF16), wd_ref[c0:c0 + chunk, :])
    o_ref[...] = acc


def _ffn(x, l, p):
    t = x.shape[0]
    f = p["w_ffn_gate"].shape[2]
    row = pl.BlockSpec((TM, D_MODEL), lambda i: (i, 0))
    once = pl.Buffered(1)
    return pl.pallas_call(
        functools.partial(_ffn_kernel, chunk=f // 2),
        grid=(t // TM,),
        in_specs=[row, pl.BlockSpec((None, 1, D_MODEL), lambda i: (l, 0, 0)),
                  pl.BlockSpec((None, D_MODEL, f), lambda i: (l, 0, 0), pipeline_mode=once),
                  pl.BlockSpec((None, D_MODEL, f), lambda i: (l, 0, 0), pipeline_mode=once),
                  pl.BlockSpec((None, f, D_MODEL), lambda i: (l, 0, 0), pipeline_mode=once)],
        out_specs=row,
        out_shape=jax.ShapeDtypeStruct((t, D_MODEL), F32),
        compiler_params=_cparams(("parallel",)),
        name="ffn",
    )(x, p["norm_ffn"], p["w_ffn_gate"], p["w_ffn_up"], p["w_ffn_down"])


def kernel(x_prompt, x_sample, cache_k, cache_v, state_gla, page_table, rel_bias, norm_mix, w_in, w_alpha2, b_alpha, qn_gain, kn_gain, lam_q1, lam_k1, lam_q2, lam_k2, subln_gain, gla_norm_gain, w_diff_out, w_gla_out, w_out, norm_ffn, w_ffn_gate, w_ffn_up, w_ffn_down):
    batch, seq, _ = x_prompt.shape
    dec_batch, ts, _ = x_sample.shape
    n_pages = page_table.shape[1]
    n_pool = cache_k.shape[1]
    past = n_pages * PAGE_SIZE
    depth = w_in.shape[0]
    gate0 = MAIN_W + GATE_RANK

    vec = lambda a: a.reshape(depth, 1, a.shape[-1])
    p = {
        "norm_mix": vec(norm_mix), "norm_ffn": vec(norm_ffn), "b_alpha": vec(b_alpha),
        "gq": vec(jnp.tile(qn_gain, (1, WA // DHA))), "gk": vec(jnp.tile(kn_gain, (1, WA // DHA))),
        "subln": vec(subln_gain), "gla_norm": vec(gla_norm_gain),
        "w_main": w_in[:, :, :MAIN_W].astype(BF16),
        "w_alr": jnp.pad(w_in[:, :, MAIN_W:gate0], ((0, 0), (0, 0), (0, LANES - GATE_RANK))).astype(BF16),
        "w_gate": w_in[:, :, gate0:].astype(BF16),
        "w_alpha2": jnp.pad(w_alpha2, ((0, 0), (0, LANES - GATE_RANK), (0, 0))).astype(BF16),
        "w_diff_out": w_diff_out.astype(BF16), "w_gla_out": w_gla_out.astype(BF16), "w_out": w_out.astype(BF16),
        "w_ffn_gate": w_ffn_gate.astype(BF16), "w_ffn_up": w_ffn_up.astype(BF16),
        "w_ffn_down": w_ffn_down.astype(BF16),
        "gsum": jnp.kron(jnp.eye(WA // DHA, dtype=F32), jnp.full((DHA, DHA), 1.0 / DHA, F32)).astype(BF16),
    }
    lamp = jnp.stack([lam_q1, lam_k1, lam_q2, lam_k2], axis=1)

    ckt = jnp.transpose(cache_k, (0, 1, 3, 4, 5, 2)).reshape(depth, n_pool, WA, PAGE_SIZE)
    cv4 = cache_v.reshape(depth, n_pool, PAGE_SIZE * HA, LANES)
    sg = state_gla.reshape(depth, dec_batch, HG // 2, 2 * DKG, DVG)
    s0p = jnp.zeros((1, batch, HG // 2, 2 * DKG, DVG), F32)

    bias_p = _bias_tables(rel_bias, 4, TK, 2 * TQ, TQ, 0, TQ, True, True)
    bias_s = _bias_tables(rel_bias, 1, 2 * ts, past + PAGE_SIZE, ts, past, 0, False, False)
    bias_s = bias_s.reshape(HA * 2 * ts, past + PAGE_SIZE)

    xp = x_prompt.reshape(batch * seq, D_MODEL)
    xs = x_sample.reshape(dec_batch * ts, D_MODEL)
    kv = ()
    sp_l, ks_l, vs_l, ss_l = [], [], [], []
    for l in range(depth):
        lam_init = 0.8 - 0.6 * math.exp(-0.3 * l)
        qa, kt_all, v4_all, qg, kg, vg, rg, la = _inproj(xp, l, p, kv, batch, seq)
        kv = (kt_all, v4_all)
        oa = _attn_prompt(qa, kt_all, v4_all, bias_p, lamp[l], l, lam_init, batch, seq)
        og, sp = _gla(qg, kg, vg, la, s0p, 0, batch, seq, batch, GLA_C, GLA_C)
        xp = _merge(xp, oa, og, rg, l, p, lam_init)
        xp = _ffn(xp, l, p)
        sp_l.append(sp)
        qa, ka, va, qg, kg, vg, rg, la = _inproj(xs, l, p)
        oa = _attn_sample(qa, ka, va, ckt, cv4, page_table, bias_s, lamp[l], l, lam_init, dec_batch, ts)
        og, ss = _gla(qg, kg, vg, la, sg, l, dec_batch, ts, GLA_SB, ts, 2 * ts)
        xs = _merge(xs, oa, og, rg, l, p, lam_init)
        xs = _ffn(xs, l, p)
        ks_l.append(ka)
        vs_l.append(va)
        ss_l.append(ss)
    kt_all, v4_all = kv
    k_prompt = jnp.transpose(kt_all.reshape(depth, batch, HA, 2, DHA, seq), (0, 1, 5, 2, 3, 4))
    v_prompt = v4_all.reshape(depth, batch, seq, HA, 2 * DHA)
    return (xp.reshape(batch, seq, D_MODEL), xs.reshape(dec_batch, ts, D_MODEL),
            k_prompt, v_prompt, jnp.stack(sp_l),
            jnp.stack(ks_l).reshape(depth, dec_batch, ts, HA, 2, DHA),
            jnp.stack(vs_l).reshape(depth, dec_batch, ts, HA, 2 * DHA),
            jnp.stack(ss_l))
```

```python
import functools
import math

import jax
import jax.numpy as jnp
from jax import lax
from jax.experimental import pallas as pl
from jax.experimental.pallas import tpu as pltpu

F32 = jnp.float32
BF16 = jnp.bfloat16

D_MODEL = 1024
HA = 4
DHA = 64
HG = 4
DKG = 64
DVG = 128
GATE_RANK = 16
GATE_NORM = 16.0
N_BUCKETS = 32
MAX_DISTANCE = 128
PAGE_SIZE = 128
EPS = 1e-6
WA = HA * 2 * DHA
WGK = HG * DKG
WGV = HG * DVG
MAIN_W = 3 * WA + 2 * WGK + 2 * WGV
LANES = 128
NEG = -1e30
LOG2E = math.log2(math.e)
LOGIT_SAFE = 60.0

TM = 512
TQ = 512
TK = 512
GLA_C = 64
GLA_SB = 8
VMEM_LIMIT = 56 * 1024 * 1024


def _cparams(sem):
    return pltpu.CompilerParams(dimension_semantics=sem, vmem_limit_bytes=VMEM_LIMIT)


def _rms(x, gain):
    ms = jnp.mean(x * x, axis=-1, keepdims=True)
    return x * lax.rsqrt(ms + EPS) * gain


def _dot(a, b):
    return jnp.dot(a, b, preferred_element_type=F32)


def _dot_nt(a, b):
    return lax.dot_general(a, b, (((1,), (1,)), ((), ())), preferred_element_type=F32)


def _dot_tn(a, b):
    return lax.dot_general(a, b, (((0,), (0,)), ((), ())), preferred_element_type=F32)


def _lam(lamp_ref, lam_init):
    a = jnp.sum(lamp_ref[0:1, :] * lamp_ref[1:2, :], axis=-1, keepdims=True)
    b = jnp.sum(lamp_ref[2:3, :] * lamp_ref[3:4, :], axis=-1, keepdims=True)
    return jnp.exp(a) - jnp.exp(b) + lam_init


def _bias_kernel(rb_ref, o_ref, *, period, off0, off_step, transposed, shifted):
    h = pl.program_id(0)
    t = pl.program_id(1)
    rows, cols = o_ref.shape
    r = lax.broadcasted_iota(jnp.int32, (rows, cols), 0)
    c = lax.broadcasted_iota(jnp.int32, (rows, cols), 1)
    if transposed:
        r, c = c, r
    d = off0 + t * off_step + jnp.bitwise_and(r, period - 1) - c
    n = jnp.maximum(d, 0)
    max_exact = N_BUCKETS // 2
    nf = jnp.maximum(n, 1).astype(F32)
    large = max_exact + (jnp.log(nf / max_exact) / math.log(MAX_DISTANCE / max_exact)
                         * (N_BUCKETS - max_exact)).astype(jnp.int32)
    large = jnp.minimum(large, N_BUCKETS - 1)
    bucket = jnp.where(n < max_exact, n, large)
    val = jnp.zeros((rows, cols), F32)
    for k in range(N_BUCKETS):
        val = jnp.where(bucket == k, rb_ref[k, h], val)
    if shifted:
        val = val - rb_ref[N_BUCKETS - 1, h]
    o_ref[...] = jnp.where(d >= 0, val * LOG2E, NEG)


def _bias_tables(rel_bias, n_t, rows, cols, period, off0, off_step, transposed, shifted):
    return pl.pallas_call(
        functools.partial(_bias_kernel, period=period, off0=off0, off_step=off_step,
                          transposed=transposed, shifted=shifted),
        grid=(HA, n_t),
        in_specs=[pl.BlockSpec(memory_space=pltpu.SMEM)],
        out_specs=pl.BlockSpec((None, None, rows, cols), lambda h, t: (h, t, 0, 0)),
        out_shape=jax.ShapeDtypeStruct((HA, n_t, rows, cols), F32),
        compiler_params=_cparams(("arbitrary", "arbitrary")),
        name="bias_tables",
    )(rel_bias)


def _inproj_kernel(*refs, prompt, n_alias):
    (x_ref, gmix_ref, w_ref, walr_ref, wa2_ref, ba_ref, gq_ref, gk_ref, gsum_ref) = refs[:9]
    (qa_ref, ka_ref, va_ref, qg_ref, kg_ref, vg_ref, rg_ref, la_ref) = refs[9 + n_alias:]
    h = _rms(x_ref[...], gmix_ref[...]).astype(BF16)
    proj = _dot(h, w_ref[...])
    gs = gsum_ref[...]
    qa = proj[:, 0:WA]
    ka = proj[:, WA:2 * WA]
    qms = _dot((qa * qa).astype(BF16), gs)
    kms = _dot((ka * ka).astype(BF16), gs)
    qa_ref[...] = (qa * lax.rsqrt(qms + EPS) * gq_ref[...] * (DHA ** -0.5 * LOG2E)).astype(BF16)
    kan = ka * lax.rsqrt(kms + EPS) * gk_ref[...]
    if prompt:
        ka_ref[...] = kan.T
        for hd in range(HA):
            va_ref[pl.ds(hd, TM, stride=HA), :] = proj[:, 2 * WA + hd * LANES:2 * WA + (hd + 1) * LANES]
    else:
        ka_ref[...] = kan
        va_ref[...] = proj[:, 2 * WA:3 * WA]
    o = 3 * WA
    qg_ref[...] = proj[:, o:o + WGK] * (DKG ** -0.5)
    kg_ref[...] = proj[:, o + WGK:o + 2 * WGK]
    o += 2 * WGK
    vg_ref[...] = proj[:, o:o + WGV]
    rg_ref[...] = proj[:, o + WGV:o + 2 * WGV]
    alr = _dot(h, walr_ref[...])
    z = _dot(alr.astype(BF16), wa2_ref[...]) + ba_ref[...]
    log_sig = jnp.minimum(z, 0.0) - jnp.log(1.0 + jnp.exp(-jnp.abs(z)))
    la_ref[...] = log_sig * (1.0 / GATE_NORM)


def _inproj(x, l, p, kv_prev=None, batch=None, seq=None):
    t = x.shape[0]
    depth = p["w_main"].shape[0]
    prompt = kv_prev is not None
    row = lambda w: pl.BlockSpec((TM, w), lambda i: (i, 0))
    lay2 = lambda w: pl.BlockSpec((None, 1, w), lambda i: (l, 0, 0))
    lay3 = lambda a, b: pl.BlockSpec((None, a, b), lambda i: (l, 0, 0), pipeline_mode=pl.Buffered(1))
    in_specs = [row(D_MODEL), lay2(D_MODEL), lay3(D_MODEL, MAIN_W), lay3(D_MODEL, LANES),
                lay3(LANES, WGK), lay2(WGK), lay2(WA), lay2(WA),
                pl.BlockSpec((WA, WA), lambda i: (0, 0))]
    args = [x, p["norm_mix"], p["w_main"], p["w_alr"], p["w_alpha2"], p["b_alpha"], p["gq"], p["gk"], p["gsum"]]
    rest = [(WGK, F32), (WGK, F32), (WGV, F32), (WGV, F32), (WGK, F32)]
    if prompt:
        spb = seq // TM
        kv_specs = [pl.BlockSpec((None, None, WA, TM), lambda i: (l, i // spb, 0, i % spb)),
                    pl.BlockSpec((None, TM * HA, LANES), lambda i: (l, i, 0))]
        kv_shapes = [jax.ShapeDtypeStruct((depth, batch, WA, seq), F32),
                     jax.ShapeDtypeStruct((depth, t * HA, LANES), F32)]
        aliases = {}
        for n, a in enumerate(kv_prev):
            in_specs.append(pl.BlockSpec(memory_space=pl.ANY))
            args.append(a)
            aliases[9 + n] = 1 + n
    else:
        kv_specs = [row(WA), row(WA)]
        kv_shapes = [jax.ShapeDtypeStruct((t, WA), F32)] * 2
        aliases = {}
    return pl.pallas_call(
        functools.partial(_inproj_kernel, prompt=prompt, n_alias=len(aliases)),
        grid=(t // TM,),
        in_specs=in_specs,
        out_specs=[row(WA)] + kv_specs + [row(w) for w, _ in rest],
        out_shape=[jax.ShapeDtypeStruct((t, WA), BF16)] + kv_shapes
                  + [jax.ShapeDtypeStruct((t, w), dt) for w, dt in rest],
        input_output_aliases=aliases,
        compiler_params=_cparams(("parallel",)),
        name="inproj",
    )(*args)


def _bound_kernel(gq_ref, gk_ref, rb_ref, o_ref):
    mq = jnp.max(jnp.abs(gq_ref[...]), axis=1, keepdims=True)
    mk = jnp.max(jnp.abs(gk_ref[...]), axis=1, keepdims=True)
    rb = rb_ref[...]
    shifted = jnp.abs(rb - rb[N_BUCKETS - 1:N_BUCKETS, :])
    bmax = jnp.max(jnp.max(shifted, axis=1, keepdims=True), axis=0, keepdims=True)
    bound = LOG2E * (1.01 * DHA ** 0.5 * mq * mk + bmax)
    o_ref[...] = jnp.broadcast_to((bound <= LOGIT_SAFE).astype(jnp.int32), o_ref.shape)


def _logit_bound_flags(qn_gain, kn_gain, rel_bias):
    depth = qn_gain.shape[0]
    return pl.pallas_call(
        _bound_kernel,
        out_shape=jax.ShapeDtypeStruct((depth, LANES), jnp.int32),
        name="logit_bound",
    )(qn_gain, kn_gain, rel_bias)


def _attn_kernel(flag_ref, lamp_ref, q_ref, kt_ref, v_ref, bias_ref, o_ref,
                 kb_ref, vtb_ref, qs_ref, m_ref, l_ref, acc_ref, sa_ref, sb_ref, *, lam_init, seq, layer):
    hd = pl.program_id(1)
    i = pl.program_id(2)

    @pl.when(i == 0)
    def _():
        for c0 in range(0, seq, TK):
            kb_ref[c0:c0 + TK, :] = kt_ref[:, c0:c0 + TK].T.astype(BF16)
            vh = v_ref[pl.ds(c0 * HA + hd, TK, stride=HA), :]
            vtb_ref[:, c0:c0 + TK] = vh.T.astype(BF16)

    lane = lax.broadcasted_iota(jnp.int32, (1, LANES), 1)
    q = q_ref[...]
    qs_ref[0:TQ, :] = jnp.where(lane < DHA, q, jnp.zeros_like(q))
    qs_ref[TQ:2 * TQ, :] = jnp.where(lane >= DHA, q, jnp.zeros_like(q))
    m_ref[...] = jnp.full(m_ref.shape, NEG, F32)
    l_ref[...] = jnp.zeros(l_ref.shape, F32)
    acc_ref[...] = jnp.zeros(acc_ref.shape, F32)
    lam = _lam(lamp_ref, lam_init)

    jl = i

    def logits(j, dst_ref):
        start = pl.multiple_of(jnp.minimum(j, jl) * TK, TK)
        dst_ref[...] = _dot_nt(kb_ref[pl.ds(start, TK), :], qs_ref[...])


    def accumulate(p, j):
        start = pl.multiple_of(j * TK, TK)
        l_ref[...] += jnp.sum(p, axis=0, keepdims=True)
        acc_ref[...] += _dot(vtb_ref[:, pl.ds(start, TK)], p.astype(BF16))

    @pl.when(flag_ref[layer, 0] == 1)
    def _():
        logits(jl, sa_ref)

        @pl.when(jl == 0)
        def _():
            accumulate(jnp.exp2(sa_ref[...] + bias_ref[0]), jl)

        @pl.when(jl >= 1)
        def _():
            logits(jl - 1, sb_ref)
            accumulate(jnp.exp2(sa_ref[...] + bias_ref[0]), jl)
            logits(0, sa_ref)
            accumulate(jnp.exp2(sb_ref[...] + bias_ref[1]), jl - 1)

        n_far = jnp.maximum(jl - 1, 0)

        def far_pair(jj, carry):
            j0 = 2 * jj
            logits(j0 + 1, sb_ref)
            accumulate(jnp.exp2(sa_ref[...]), j0)

            @pl.when(j0 + 1 < n_far)
            def _():
                logits(j0 + 2, sa_ref)
                accumulate(jnp.exp2(sb_ref[...]), j0 + 1)

            return carry

        lax.fori_loop(0, (n_far + 1) // 2, far_pair, 0)

    @pl.when(flag_ref[layer, 0] == 0)
    def _():
        def update(src_ref, j):
            table = jnp.minimum(jl - j, 2)
            start = pl.multiple_of(j * TK, TK)
            s = src_ref[...] + bias_ref[table]
            m_prev = m_ref[...]
            m_new = jnp.maximum(m_prev, jnp.max(s, axis=0, keepdims=True))
            alpha = jnp.exp2(m_prev - m_new)
            p = jnp.exp2(s - m_new)
            l_ref[...] = alpha * l_ref[...] + jnp.sum(p, axis=0, keepdims=True)
            acc_ref[...] = alpha * acc_ref[...] + _dot(vtb_ref[:, pl.ds(start, TK)], p.astype(BF16))
            m_ref[...] = m_new

        logits(0, sa_ref)

        def pair_body(jj, carry):
            j0 = 2 * jj
            logits(j0 + 1, sb_ref)
            update(sa_ref, j0)

            @pl.when(j0 + 1 <= jl)
            def _():
                logits(j0 + 2, sa_ref)
                update(sb_ref, j0 + 1)

            return carry

        lax.fori_loop(0, jl // 2 + 1, pair_body, 0)

    o = acc_ref[...] * (1.0 / l_ref[...])
    o_ref[...] = (o[:, 0:TQ] - lam * o[:, TQ:2 * TQ]).T


def _attn_prompt(qa, kt_all, v4_all, bias, flags, lamp, l, lam_init, batch, seq):
    depth = kt_all.shape[0]
    q3 = qa.reshape(batch, seq, WA)
    v3 = v4_all.reshape(depth * batch, seq * HA, LANES)
    blk = pl.BlockSpec((None, TQ, LANES), lambda b, h, i: (b, i, h))
    out = pl.pallas_call(
        functools.partial(_attn_kernel, lam_init=lam_init, seq=seq, layer=l),
        grid=(batch, HA, seq // TQ),
        in_specs=[pl.BlockSpec(memory_space=pltpu.SMEM),
                  pl.BlockSpec((4, DHA), lambda b, h, i: (0, 0)),
                  blk,
                  pl.BlockSpec((None, None, LANES, seq), lambda b, h, i: (l, b, h, 0)),
                  pl.BlockSpec((None, seq * HA, LANES), lambda b, h, i: (l * batch + b, 0, 0)),
                  pl.BlockSpec((None, 3, TK, 2 * TQ), lambda b, h, i: (h, 0, 0, 0),
                               pipeline_mode=pl.Buffered(1))],
        out_specs=blk,
        out_shape=jax.ShapeDtypeStruct((batch, seq, WA), F32),
        scratch_shapes=[pltpu.VMEM((seq, LANES), BF16), pltpu.VMEM((LANES, seq), BF16),
                        pltpu.VMEM((2 * TQ, LANES), BF16),
                        pltpu.VMEM((1, 2 * TQ), F32), pltpu.VMEM((1, 2 * TQ), F32),
                        pltpu.VMEM((LANES, 2 * TQ), F32),
                        pltpu.VMEM((TK, 2 * TQ), F32), pltpu.VMEM((TK, 2 * TQ), F32)],
        compiler_params=_cparams(("parallel", "parallel", "arbitrary")),
        name="attn_prompt",
    )(flags, lamp, q3, kt_all, v3, bias)
    return out.reshape(batch * seq, WA)


def _attn_sample_kernel(pt_ref, lamp_ref, q_ref, kn_ref, vn_ref, bias_ref, *rest, n_pages, ts, lam_init):
    kp = rest[:n_pages]
    vp = rest[n_pages:2 * n_pages]
    o_ref = rest[2 * n_pages]
    s_ref = rest[2 * n_pages + 1]
    nr = HA * 2 * ts
    past = n_pages * PAGE_SIZE

    q = q_ref[...].astype(F32)
    qt = jnp.concatenate([q] * (2 * HA), axis=0)
    r = lax.broadcasted_iota(jnp.int32, (nr, WA), 0)
    c = lax.broadcasted_iota(jnp.int32, (nr, WA), 1)
    keep = lax.shift_right_logical(c, int(math.log2(DHA))) == lax.shift_right_logical(r, int(math.log2(ts)))
    qbd = jnp.where(keep, qt, 0.0).astype(BF16)
    zpad = jnp.zeros((PAGE_SIZE - ts, WA), F32)
    knp = jnp.concatenate([kn_ref[...], zpad], axis=0).astype(BF16)
    vnp = jnp.concatenate([vn_ref[...], zpad], axis=0).astype(BF16)

    for pg in range(n_pages):
        s_ref[:, pg * PAGE_SIZE:(pg + 1) * PAGE_SIZE] = _dot(qbd, kp[pg][...].astype(BF16))
    s_ref[:, past:past + PAGE_SIZE] = _dot_nt(qbd, knp)

    s = s_ref[...] + bias_ref[...]
    m = jnp.max(s, axis=-1, keepdims=True)
    p = jnp.exp2(s - m)
    l = jnp.sum(p, axis=-1, keepdims=True)
    pb = p.astype(BF16)
    lam = _lam(lamp_ref, lam_init)
    for h in range(HA):
        rs = slice(h * 2 * ts, (h + 1) * 2 * ts)
        cs = slice(h * LANES, (h + 1) * LANES)
        acc = _dot(pb[rs, past:past + PAGE_SIZE], vnp[:, cs])
        for pg in range(n_pages):
            vh = vp[pg][pl.ds(h, PAGE_SIZE, stride=HA), :].astype(BF16)
            acc = acc + _dot(pb[rs, pg * PAGE_SIZE:(pg + 1) * PAGE_SIZE], vh)
        acc = acc / l[rs]
        o_ref[:, cs] = acc[0:ts, :] - lam * acc[ts:2 * ts, :]


def _attn_sample(qa, ka, va, ckt, cv4, page_table, bias, lamp, l, lam_init, dec_batch, ts):
    n_pages = page_table.shape[1]
    nr = HA * 2 * ts
    width = n_pages * PAGE_SIZE + PAGE_SIZE
    tok = pl.BlockSpec((None, ts, WA), lambda b, pt: (b, 0, 0))
    page = lambda pg: pl.BlockSpec((None, None, HA * PAGE_SIZE, LANES), lambda b, pt: (l, pt[b, pg], 0, 0))
    grid_spec = pltpu.PrefetchScalarGridSpec(
        num_scalar_prefetch=1,
        grid=(dec_batch,),
        in_specs=[pl.BlockSpec((4, DHA), lambda b, pt: (0, 0)), tok, tok, tok,
                  pl.BlockSpec((nr, width), lambda b, pt: (0, 0))]
                 + [page(pg) for pg in range(n_pages)] * 2,
        out_specs=tok,
        scratch_shapes=[pltpu.VMEM((nr, width), F32)],
    )
    out = pl.pallas_call(
        functools.partial(_attn_sample_kernel, n_pages=n_pages, ts=ts, lam_init=lam_init),
        grid_spec=grid_spec,
        out_shape=jax.ShapeDtypeStruct((dec_batch, ts, WA), F32),
        compiler_params=_cparams(("parallel",)),
        name="attn_sample",
    )(page_table, lamp, qa.reshape(dec_batch, ts, WA), ka.reshape(dec_batch, ts, WA),
      va.reshape(dec_batch, ts, WA), bias, *([ckt] * n_pages), *([cv4] * n_pages))
    return out.reshape(dec_batch * ts, WA)


def _split3(x):
    x1 = x.astype(BF16)
    r1 = x - x1.astype(F32)
    x2 = r1.astype(BF16)
    x3 = (r1 - x2.astype(F32)).astype(BF16)
    return x1, x2, x3


def _gla_kernel(q_ref, k_ref, v_ref, g_ref, s0_ref, tril_ref, o_ref, sout_ref, st_scr, *, nb, c_in, c):
    ci = pl.program_id(1)
    units = [(b, pr) for b in range(nb) for pr in range(HG // 2)]

    @pl.when(ci == 0)
    def _():
        for b, pr in units:
            st_scr[b, pr] = s0_ref[b, pr].T

    def pad(x):
        if c_in == c:
            return x
        return jnp.concatenate([x, jnp.zeros((c - c_in, x.shape[1]), x.dtype)], axis=0)

    lane = lax.broadcasted_iota(jnp.int32, (1, LANES), 1)
    head0 = lane < DKG
    tt = lax.broadcasted_iota(jnp.int32, (2 * c, c), 0)
    ss = lax.broadcasted_iota(jnp.int32, (2 * c, c), 1)
    causal = jnp.bitwise_and(tt, c - 1) >= ss
    mid = c // 2 - 1

    def heads_on_rows(x):
        return jnp.concatenate([jnp.where(head0, x, 0.0), jnp.where(head0, 0.0, x)], axis=0).astype(BF16)

    g_all = jnp.concatenate([pad(g_ref[b]) for b in range(nb)], axis=1)
    tril = tril_ref[...]
    g3 = _split3(g_all)
    bc_all = _dot(tril, g3[0]) + _dot(tril, g3[1]) + _dot(tril, g3[2])

    prep = []
    for b, pr in units:
        ks = slice(pr * LANES, (pr + 1) * LANES)
        q = pad(q_ref[b, :, ks])
        k = pad(k_ref[b, :, ks])
        bcum = bc_all[:, b * WGK + pr * LANES:b * WGK + (pr + 1) * LANES]
        bm = bcum[mid:mid + 1, :]
        bl = bcum[c - 1:c, :]
        prep.append(dict(
            qt=heads_on_rows(q * jnp.exp(bcum - bm)),
            qi=heads_on_rows(q * jnp.exp(bcum)),
            kt=(k * jnp.exp(bm - bcum)).astype(BF16),
            kd=(k * jnp.exp(bl - bcum)).astype(BF16),
            decay=jnp.exp(bl),
            v2=pad(v_ref[b, :, 2 * pr * DVG:2 * (pr + 1) * DVG]).astype(BF16)))

    amat = [jnp.where(causal, _dot_nt(u["qt"], u["kt"]), 0.0).astype(BF16) for u in prep]
    inter = [_dot_nt(u["qi"], st_scr[b, pr].astype(BF16)) for u, (b, pr) in zip(prep, units)]
    for u, a, it, (b, pr) in zip(prep, amat, inter, units):
        for hh in range(2):
            o = _dot(a[hh * c:(hh + 1) * c, :], u["v2"][:, hh * DVG:(hh + 1) * DVG]) + it[hh * c:(hh + 1) * c, :]
            o_ref[b, :, (2 * pr + hh) * DVG:(2 * pr + hh + 1) * DVG] = o[0:c_in, :]
    for u, (b, pr) in zip(prep, units):
        upd = _dot_tn(u["v2"], u["kd"])
        st_scr[b, pr] = st_scr[b, pr] * u["decay"] + jnp.where(head0, upd[0:DVG, :], upd[DVG:2 * DVG, :])

    @pl.when(ci == pl.num_programs(1) - 1)
    def _():
        for b, pr in units:
            sout_ref[b, pr] = st_scr[b, pr].T


def _gla(qg, kg, vg, la, s0, s0_layer, batch, seq, nb, c_in, c):
    n_chunks = seq // c_in
    tril = jnp.tril(jnp.ones((c, c), BF16))
    tok = lambda w: pl.BlockSpec((nb, c_in, w), lambda bi, ci: (bi, ci, 0))
    st = pl.BlockSpec((nb, HG // 2, 2 * DKG, DVG), lambda bi, ci: (bi, 0, 0, 0))
    st_in = pl.BlockSpec((None, nb, HG // 2, 2 * DKG, DVG), lambda bi, ci: (s0_layer, bi, 0, 0, 0))
    o, s_new = pl.pallas_call(
        functools.partial(_gla_kernel, nb=nb, c_in=c_in, c=c),
        grid=(batch // nb, n_chunks),
        in_specs=[tok(WGK), tok(WGK), tok(WGV), tok(WGK), st_in,
                  pl.BlockSpec((c, c), lambda bi, ci: (0, 0))],
        out_specs=[tok(WGV), st],
        out_shape=[jax.ShapeDtypeStruct((batch, seq, WGV), F32),
                   jax.ShapeDtypeStruct((batch, HG // 2, 2 * DKG, DVG), F32)],
        scratch_shapes=[pltpu.VMEM((nb, HG // 2, 2 * DKG, DVG), F32)],
        compiler_params=_cparams(("parallel", "arbitrary")),
        name="gla",
    )(qg.reshape(batch, seq, WGK), kg.reshape(batch, seq, WGK), vg.reshape(batch, seq, WGV),
      la.reshape(batch, seq, WGK), s0, tril)
    return o.reshape(batch * seq, WGV), s_new.reshape(batch, HG, DKG, DVG)


def _merge_kernel(x_ref, oa_ref, og_ref, rg_ref, gmix_ref, subln_ref, glan_ref,
                  wgate_ref, wdo_ref, wgo_ref, wout_ref, o_ref, ya_ref, yg_ref, *, lam_init):
    x = x_ref[...]
    h = _rms(x, gmix_ref[...]).astype(BF16)
    gate = jax.nn.sigmoid(_dot(h, wgate_ref[...]))
    for hd in range(HA):
        cs = slice(hd * LANES, (hd + 1) * LANES)
        ya_ref[:, cs] = (_rms(oa_ref[:, cs], subln_ref[...]) * (1.0 - lam_init)).astype(BF16)
        yg_ref[:, cs] = (_rms(og_ref[:, cs], glan_ref[...]) * jax.nn.silu(rg_ref[:, cs])).astype(BF16)
    ya = _dot(ya_ref[...], wdo_ref[...])
    yg = _dot(yg_ref[...], wgo_ref[...])
    mix = gate[:, 0:D_MODEL] * ya + gate[:, D_MODEL:2 * D_MODEL] * yg
    o_ref[...] = x + _dot(mix.astype(BF16), wout_ref[...])


def _merge(x, oa, og, rg, l, p, lam_init):
    t = x.shape[0]
    row = lambda w: pl.BlockSpec((TM, w), lambda i: (i, 0))
    lay2 = lambda w: pl.BlockSpec((None, 1, w), lambda i: (l, 0, 0))
    lay3 = lambda a, b: pl.BlockSpec((None, a, b), lambda i: (l, 0, 0), pipeline_mode=pl.Buffered(1))
    return pl.pallas_call(
        functools.partial(_merge_kernel, lam_init=lam_init),
        grid=(t // TM,),
        in_specs=[row(D_MODEL), row(WA), row(WGV), row(WGV), lay2(D_MODEL), lay2(2 * DHA), lay2(DVG),
                  lay3(D_MODEL, 2 * D_MODEL), lay3(WA, D_MODEL), lay3(WGV, D_MODEL), lay3(D_MODEL, D_MODEL)],
        out_specs=row(D_MODEL),
        out_shape=jax.ShapeDtypeStruct((t, D_MODEL), F32),
        scratch_shapes=[pltpu.VMEM((TM, WA), BF16), pltpu.VMEM((TM, WGV), BF16)],
        compiler_params=_cparams(("parallel",)),
        name="merge",
    )(x, oa, og, rg, p["norm_mix"], p["subln"], p["gla_norm"],
      p["w_gate"], p["w_diff_out"], p["w_gla_out"], p["w_out"])


def _ffn_kernel(x_ref, g_ref, wg_ref, wu_ref, wd_ref, o_ref, *, chunk):
    x = x_ref[...]
    h = _rms(x, g_ref[...]).astype(BF16)
    acc = x
    for c0 in range(0, wg_ref.shape[1], chunk):
        a = jax.nn.silu(_dot(h, wg_ref[:, c0:c0 + chunk])) * _dot(h, wu_ref[:, c0:c0 + chunk])
        acc = acc + _dot(a.astype(BF16), wd_ref[c0:c0 + chunk, :])
    o_ref[...] = acc


def _ffn(x, l, p):
    t = x.shape[0]
    f = p["w_ffn_gate"].shape[2]
    row = pl.BlockSpec((TM, D_MODEL), lambda i: (i, 0))
    once = pl.Buffered(1)
    return pl.pallas_call(
        functools.partial(_ffn_kernel, chunk=f // 2),
        grid=(t // TM,),
        in_specs=[row, pl.BlockSpec((None, 1, D_MODEL), lambda i: (l, 0, 0)),
                  pl.BlockSpec((None, D_MODEL, f), lambda i: (l, 0, 0), pipeline_mode=once),
                  pl.BlockSpec((None, D_MODEL, f), lambda i: (l, 0, 0), pipeline_mode=once),
                  pl.BlockSpec((None, f, D_MODEL), lambda i: (l, 0, 0), pipeline_mode=once)],
        out_specs=row,
        out_shape=jax.ShapeDtypeStruct((t, D_MODEL), F32),
        compiler_params=_cparams(("parallel",)),
        name="ffn",
    )(x, p["norm_ffn"], p["w_ffn_gate"], p["w_ffn_up"], p["w_ffn_down"])


def kernel(x_prompt, x_sample, cache_k, cache_v, state_gla, page_table, rel_bias, norm_mix, w_in, w_alpha2, b_alpha, qn_gain, kn_gain, lam_q1, lam_k1, lam_q2, lam_k2, subln_gain, gla_norm_gain, w_diff_out, w_gla_out, w_out, norm_ffn, w_ffn_gate, w_ffn_up, w_ffn_down):
    batch, seq, _ = x_prompt.shape
    dec_batch, ts, _ = x_sample.shape
    n_pages = page_table.shape[1]
    n_pool = cache_k.shape[1]
    past = n_pages * PAGE_SIZE
    depth = w_in.shape[0]
    gate0 = MAIN_W + GATE_RANK

    vec = lambda a: a.reshape(depth, 1, a.shape[-1])
    p = {
        "norm_mix": vec(norm_mix), "norm_ffn": vec(norm_ffn), "b_alpha": vec(b_alpha),
        "gq": vec(jnp.tile(qn_gain, (1, WA // DHA))), "gk": vec(jnp.tile(kn_gain, (1, WA // DHA))),
        "subln": vec(subln_gain), "gla_norm": vec(gla_norm_gain),
        "w_main": w_in[:, :, :MAIN_W].astype(BF16),
        "w_alr": jnp.pad(w_in[:, :, MAIN_W:gate0], ((0, 0), (0, 0), (0, LANES - GATE_RANK))).astype(BF16),
        "w_gate": w_in[:, :, gate0:].astype(BF16),
        "w_alpha2": jnp.pad(w_alpha2, ((0, 0), (0, LANES - GATE_RANK), (0, 0))).astype(BF16),
        "w_diff_out": w_diff_out.astype(BF16), "w_gla_out": w_gla_out.astype(BF16), "w_out": w_out.astype(BF16),
        "w_ffn_gate": w_ffn_gate.astype(BF16), "w_ffn_up": w_ffn_up.astype(BF16),
        "w_ffn_down": w_ffn_down.astype(BF16),
        "gsum": jnp.kron(jnp.eye(WA // DHA, dtype=F32), jnp.full((DHA, DHA), 1.0 / DHA, F32)).astype(BF16),
    }
    lamp = jnp.stack([lam_q1, lam_k1, lam_q2, lam_k2], axis=1)

    ckt = jnp.transpose(cache_k, (0, 1, 3, 4, 5, 2)).reshape(depth, n_pool, WA, PAGE_SIZE)
    cv4 = cache_v.reshape(depth, n_pool, PAGE_SIZE * HA, LANES)
    sg = state_gla.reshape(depth, dec_batch, HG // 2, 2 * DKG, DVG)
    s0p = jnp.zeros((1, batch, HG // 2, 2 * DKG, DVG), F32)

    bias_p = _bias_tables(rel_bias, 3, TK, 2 * TQ, TQ, 0, TK, True, True)
    bias_s = _bias_tables(rel_bias, 1, 2 * ts, past + PAGE_SIZE, ts, past, 0, False, False)
    bias_s = bias_s.reshape(HA * 2 * ts, past + PAGE_SIZE)
    flags = _logit_bound_flags(qn_gain, kn_gain, rel_bias)

    xp = x_prompt.reshape(batch * seq, D_MODEL)
    xs = x_sample.reshape(dec_batch * ts, D_MODEL)
    kv = ()
    sp_l, ks_l, vs_l, ss_l = [], [], [], []
    for l in range(depth):
        lam_init = 0.8 - 0.6 * math.exp(-0.3 * l)
        qa, kt_all, v4_all, qg, kg, vg, rg, la = _inproj(xp, l, p, kv, batch, seq)
        kv = (kt_all, v4_all)
        oa = _attn_prompt(qa, kt_all, v4_all, bias_p, flags, lamp[l], l, lam_init, batch, seq)
        og, sp = _gla(qg, kg, vg, la, s0p, 0, batch, seq, batch, GLA_C, GLA_C)
        xp = _merge(xp, oa, og, rg, l, p, lam_init)
        xp = _ffn(xp, l, p)
        sp_l.append(sp)
        qa, ka, va, qg, kg, vg, rg, la = _inproj(xs, l, p)
        oa = _attn_sample(qa, ka, va, ckt, cv4, page_table, bias_s, lamp[l], l, lam_init, dec_batch, ts)
        og, ss = _gla(qg, kg, vg, la, sg, l, dec_batch, ts, GLA_SB, ts, 2 * ts)
        xs = _merge(xs, oa, og, rg, l, p, lam_init)
        xs = _ffn(xs, l, p)
        ks_l.append(ka)
        vs_l.append(va)
        ss_l.append(ss)
    kt_all, v4_all = kv
    k_prompt = jnp.transpose(kt_all.reshape(depth, batch, HA, 2, DHA, seq), (0, 1, 5, 2, 3, 4))
    v_prompt = v4_all.reshape(depth, batch, seq, HA, 2 * DHA)
    return (xp.reshape(batch, seq, D_MODEL), xs.reshape(dec_batch, ts, D_MODEL),
            k_prompt, v_prompt, jnp.stack(sp_l),
            jnp.stack(ks_l).reshape(depth, dec_batch, ts, HA, 2, DHA),
            jnp.stack(vs_l).reshape(depth, dec_batch, ts, HA, 2 * DHA),
            jnp.stack(ss_l))
```

```python
import functools
import math

import jax
import jax.numpy as jnp
from jax import lax
from jax.experimental import pallas as pl
from jax.experimental.pallas import tpu as pltpu

F32 = jnp.float32
BF16 = jnp.bfloat16

D_MODEL = 1024
HA = 4
DHA = 64
HG = 4
DKG = 64
DVG = 128
GATE_RANK = 16
GATE_NORM = 16.0
N_BUCKETS = 32
MAX_DISTANCE = 128
PAGE_SIZE = 128
EPS = 1e-6
WA = HA * 2 * DHA
WGK = HG * DKG
WGV = HG * DVG
MAIN_W = 3 * WA + 2 * WGK + 2 * WGV
LANES = 128
NEG = -1e30
LOG2E = math.log2(math.e)
GLA_RANGE_SAFE = 80.0
LOGIT_SAFE = 60.0

TM = 512
TQ = 512
TK = 512
GLA_C = 64
GLA_SB = 8
VMEM_LIMIT = 56 * 1024 * 1024


def _cparams(sem):
    return pltpu.CompilerParams(dimension_semantics=sem, vmem_limit_bytes=VMEM_LIMIT)


def _rms(x, gain):
    ms = jnp.mean(x * x, axis=-1, keepdims=True)
    return x * lax.rsqrt(ms + EPS) * gain


def _dot(a, b):
    return jnp.dot(a, b, preferred_element_type=F32)


def _dot_nt(a, b):
    return lax.dot_general(a, b, (((1,), (1,)), ((), ())), preferred_element_type=F32)


def _dot_tn(a, b):
    return lax.dot_general(a, b, (((0,), (0,)), ((), ())), preferred_element_type=F32)


def _lam(lamp_ref, lam_init):
    a = jnp.sum(lamp_ref[0:1, :] * lamp_ref[1:2, :], axis=-1, keepdims=True)
    b = jnp.sum(lamp_ref[2:3, :] * lamp_ref[3:4, :], axis=-1, keepdims=True)
    return jnp.exp(a) - jnp.exp(b) + lam_init


def _bias_kernel(rb_ref, o_ref, *, period, off0, off_step, transposed, shifted):
    h = pl.program_id(0)
    t = pl.program_id(1)
    rows, cols = o_ref.shape
    r = lax.broadcasted_iota(jnp.int32, (rows, cols), 0)
    c = lax.broadcasted_iota(jnp.int32, (rows, cols), 1)
    if transposed:
        r, c = c, r
    d = off0 + t * off_step + jnp.bitwise_and(r, period - 1) - c
    n = jnp.maximum(d, 0)
    max_exact = N_BUCKETS // 2
    nf = jnp.maximum(n, 1).astype(F32)
    large = max_exact + (jnp.log(nf / max_exact) / math.log(MAX_DISTANCE / max_exact)
                         * (N_BUCKETS - max_exact)).astype(jnp.int32)
    large = jnp.minimum(large, N_BUCKETS - 1)
    bucket = jnp.where(n < max_exact, n, large)
    val = jnp.zeros((rows, cols), F32)
    for k in range(N_BUCKETS):
        val = jnp.where(bucket == k, rb_ref[k, h], val)
    if shifted:
        val = val - rb_ref[N_BUCKETS - 1, h]
    o_ref[...] = jnp.where(d >= 0, val * LOG2E, NEG)


def _bias_tables(rel_bias, n_t, rows, cols, period, off0, off_step, transposed, shifted):
    return pl.pallas_call(
        functools.partial(_bias_kernel, period=period, off0=off0, off_step=off_step,
                          transposed=transposed, shifted=shifted),
        grid=(HA, n_t),
        in_specs=[pl.BlockSpec(memory_space=pltpu.SMEM)],
        out_specs=pl.BlockSpec((None, None, rows, cols), lambda h, t: (h, t, 0, 0)),
        out_shape=jax.ShapeDtypeStruct((HA, n_t, rows, cols), F32),
        compiler_params=_cparams(("arbitrary", "arbitrary")),
        name="bias_tables",
    )(rel_bias)


def _inproj_kernel(*refs, prompt, n_alias):
    (x_ref, gmix_ref, w_ref, walr_ref, wa2_ref, ba_ref, gq_ref, gk_ref, gsum_ref) = refs[:9]
    (qa_ref, ka_ref, va_ref, qg_ref, kg_ref, vg_ref, rg_ref, la_ref) = refs[9 + n_alias:]
    h = _rms(x_ref[...], gmix_ref[...]).astype(BF16)
    proj = _dot(h, w_ref[...])
    gs = gsum_ref[...]
    qa = proj[:, 0:WA]
    ka = proj[:, WA:2 * WA]
    qms = _dot((qa * qa).astype(BF16), gs)
    kms = _dot((ka * ka).astype(BF16), gs)
    qa_ref[...] = (qa * lax.rsqrt(qms + EPS) * gq_ref[...] * (DHA ** -0.5 * LOG2E)).astype(BF16)
    kan = ka * lax.rsqrt(kms + EPS) * gk_ref[...]
    if prompt:
        ka_ref[...] = kan.T
        for hd in range(HA):
            va_ref[pl.ds(hd, TM, stride=HA), :] = proj[:, 2 * WA + hd * LANES:2 * WA + (hd + 1) * LANES]
    else:
        ka_ref[...] = kan
        va_ref[...] = proj[:, 2 * WA:3 * WA]
    o = 3 * WA
    qg_ref[...] = proj[:, o:o + WGK] * (DKG ** -0.5)
    kg_ref[...] = proj[:, o + WGK:o + 2 * WGK]
    o += 2 * WGK
    vg_ref[...] = proj[:, o:o + WGV]
    rg_ref[...] = proj[:, o + WGV:o + 2 * WGV]
    alr = _dot(h, walr_ref[...])
    z = _dot(alr.astype(BF16), wa2_ref[...]) + ba_ref[...]
    log_sig = jnp.minimum(z, 0.0) - jnp.log(1.0 + jnp.exp(-jnp.abs(z)))
    la_ref[...] = log_sig * (1.0 / GATE_NORM)


def _inproj(x, l, p, kv_prev=None, batch=None, seq=None):
    t = x.shape[0]
    depth = p["w_main"].shape[0]
    prompt = kv_prev is not None
    row = lambda w: pl.BlockSpec((TM, w), lambda i: (i, 0))
    lay2 = lambda w: pl.BlockSpec((None, 1, w), lambda i: (l, 0, 0))
    lay3 = lambda a, b: pl.BlockSpec((None, a, b), lambda i: (l, 0, 0), pipeline_mode=pl.Buffered(1))
    in_specs = [row(D_MODEL), lay2(D_MODEL), lay3(D_MODEL, MAIN_W), lay3(D_MODEL, LANES),
                lay3(LANES, WGK), lay2(WGK), lay2(WA), lay2(WA),
                pl.BlockSpec((WA, WA), lambda i: (0, 0))]
    args = [x, p["norm_mix"], p["w_main"], p["w_alr"], p["w_alpha2"], p["b_alpha"], p["gq"], p["gk"], p["gsum"]]
    rest = [(WGK, F32), (WGK, F32), (WGV, F32), (WGV, F32), (WGK, F32)]
    if prompt:
        spb = seq // TM
        kv_specs = [pl.BlockSpec((None, None, WA, TM), lambda i: (l, i // spb, 0, i % spb)),
                    pl.BlockSpec((None, TM * HA, LANES), lambda i: (l, i, 0))]
        kv_shapes = [jax.ShapeDtypeStruct((depth, batch, WA, seq), F32),
                     jax.ShapeDtypeStruct((depth, t * HA, LANES), F32)]
        aliases = {}
        for n, a in enumerate(kv_prev):
            in_specs.append(pl.BlockSpec(memory_space=pl.ANY))
            args.append(a)
            aliases[9 + n] = 1 + n
    else:
        kv_specs = [row(WA), row(WA)]
        kv_shapes = [jax.ShapeDtypeStruct((t, WA), F32)] * 2
        aliases = {}
    return pl.pallas_call(
        functools.partial(_inproj_kernel, prompt=prompt, n_alias=len(aliases)),
        grid=(t // TM,),
        in_specs=in_specs,
        out_specs=[row(WA)] + kv_specs + [row(w) for w, _ in rest],
        out_shape=[jax.ShapeDtypeStruct((t, WA), BF16)] + kv_shapes
                  + [jax.ShapeDtypeStruct((t, w), dt) for w, dt in rest],
        input_output_aliases=aliases,
        compiler_params=_cparams(("parallel",)),
        name="inproj",
    )(*args)


def _bound_kernel(gq_ref, gk_ref, rb_ref, o_ref):
    mq = jnp.max(jnp.abs(gq_ref[...]), axis=1, keepdims=True)
    mk = jnp.max(jnp.abs(gk_ref[...]), axis=1, keepdims=True)
    rb = rb_ref[...]
    shifted = jnp.abs(rb - rb[N_BUCKETS - 1:N_BUCKETS, :])
    bmax = jnp.max(jnp.max(shifted, axis=1, keepdims=True), axis=0, keepdims=True)
    bound = LOG2E * (1.01 * DHA ** 0.5 * mq * mk + bmax)
    o_ref[...] = jnp.broadcast_to((bound <= LOGIT_SAFE).astype(jnp.int32), o_ref.shape)


def _logit_bound_flags(qn_gain, kn_gain, rel_bias):
    depth = qn_gain.shape[0]
    return pl.pallas_call(
        _bound_kernel,
        out_shape=jax.ShapeDtypeStruct((depth, LANES), jnp.int32),
        name="logit_bound",
    )(qn_gain, kn_gain, rel_bias)


def _attn_kernel(flag_ref, lamp_ref, q_ref, kt_ref, v_ref, bias_ref, o_ref,
                 kb_ref, vtb_ref, qs_ref, m_ref, l_ref, acc_ref, sa_ref, sb_ref, *, lam_init, seq, layer):
    hd = pl.program_id(1)
    nq = seq // TQ
    bounded = flag_ref[layer, 0] == 1

    for c0 in range(0, seq, TK):
        kb_ref[c0:c0 + TK, :] = kt_ref[:, c0:c0 + TK].T.astype(BF16)
        vh = v_ref[pl.ds(c0 * HA + hd, TK, stride=HA), :]
        vtb_ref[:, c0:c0 + TK] = vh.T.astype(BF16)

    lane = lax.broadcasted_iota(jnp.int32, (1, LANES), 1)
    lam = _lam(lamp_ref, lam_init)

    def logits(j, dst_ref):
        start = pl.multiple_of(j * TK, TK)
        dst_ref[...] = _dot_nt(kb_ref[pl.ds(start, TK), :], qs_ref[...])

    def start_block(i):
        q = q_ref[pl.ds(pl.multiple_of(i * TQ, TQ), TQ), :]
        qs_ref[0:TQ, :] = jnp.where(lane < DHA, q, jnp.zeros_like(q))
        qs_ref[TQ:2 * TQ, :] = jnp.where(lane >= DHA, q, jnp.zeros_like(q))
        m_ref[...] = jnp.full(m_ref.shape, NEG, F32)
        l_ref[...] = jnp.zeros(l_ref.shape, F32)
        acc_ref[...] = jnp.zeros(acc_ref.shape, F32)
        logits(jnp.where(bounded, i, 0), sa_ref)

    start_block(0)


    def accumulate(p, j):
        start = pl.multiple_of(j * TK, TK)
        l_ref[...] += jnp.sum(p, axis=0, keepdims=True)
        acc_ref[...] += _dot(vtb_ref[:, pl.ds(start, TK)], p.astype(BF16))

    def q_block(i, carry):
        jl = i

        @pl.when(bounded)
        def _():
            @pl.when(jl == 0)
            def _():
                accumulate(jnp.exp2(sa_ref[...] + bias_ref[0]), jl)

            @pl.when(jl >= 1)
            def _():
                logits(jl - 1, sb_ref)
                accumulate(jnp.exp2(sa_ref[...] + bias_ref[0]), jl)
                logits(0, sa_ref)
                accumulate(jnp.exp2(sb_ref[...] + bias_ref[1]), jl - 1)

            n_far = jnp.maximum(jl - 1, 0)

            def far_pair(jj, carry):
                j0 = 2 * jj
                logits(jnp.minimum(j0 + 1, jl), sb_ref)
                accumulate(jnp.exp2(sa_ref[...]), j0)

                @pl.when(j0 + 1 < n_far)
                def _():
                    logits(jnp.minimum(j0 + 2, jl), sa_ref)
                    accumulate(jnp.exp2(sb_ref[...]), j0 + 1)

                return carry

            lax.fori_loop(0, (n_far + 1) // 2, far_pair, 0)

        @pl.when(jnp.logical_not(bounded))
        def _():
            def update(src_ref, j):
                table = jnp.minimum(jl - j, 2)
                start = pl.multiple_of(j * TK, TK)
                s = src_ref[...] + bias_ref[table]
                m_prev = m_ref[...]
                m_new = jnp.maximum(m_prev, jnp.max(s, axis=0, keepdims=True))
                alpha = jnp.exp2(m_prev - m_new)
                p = jnp.exp2(s - m_new)
                l_ref[...] = alpha * l_ref[...] + jnp.sum(p, axis=0, keepdims=True)
                acc_ref[...] = alpha * acc_ref[...] + _dot(vtb_ref[:, pl.ds(start, TK)], p.astype(BF16))
                m_ref[...] = m_new

            def pair_body(jj, carry):
                j0 = 2 * jj
                logits(jnp.minimum(j0 + 1, jl), sb_ref)
                update(sa_ref, j0)

                @pl.when(j0 + 1 <= jl)
                def _():
                    logits(jnp.minimum(j0 + 2, jl), sa_ref)
                    update(sb_ref, j0 + 1)

                return carry

            lax.fori_loop(0, jl // 2 + 1, pair_body, 0)

        o = acc_ref[...] * (1.0 / l_ref[...])
        o_ref[pl.ds(pl.multiple_of(i * TQ, TQ), TQ), :] = (o[:, 0:TQ] - lam * o[:, TQ:2 * TQ]).T
        start_block(jnp.minimum(i + 1, nq - 1))
        return carry

    lax.fori_loop(0, nq, q_block, 0)


def _attn_prompt(qa, kt_all, v4_all, bias, flags, lamp, l, lam_init, batch, seq):
    depth = kt_all.shape[0]
    q3 = qa.reshape(batch, seq, WA)
    v3 = v4_all.reshape(depth * batch, seq * HA, LANES)
    blk = pl.BlockSpec((None, seq, LANES), lambda b, h: (b, 0, h))
    out = pl.pallas_call(
        functools.partial(_attn_kernel, lam_init=lam_init, seq=seq, layer=l),
        grid=(batch, HA),
        in_specs=[pl.BlockSpec(memory_space=pltpu.SMEM),
                  pl.BlockSpec((4, DHA), lambda b, h: (0, 0)),
                  blk,
                  pl.BlockSpec((None, None, LANES, seq), lambda b, h: (l, b, h, 0)),
                  pl.BlockSpec((None, seq * HA, LANES), lambda b, h: (l * batch + b, 0, 0)),
                  pl.BlockSpec((None, 3, TK, 2 * TQ), lambda b, h: (h, 0, 0, 0),
                               pipeline_mode=pl.Buffered(1))],
        out_specs=blk,
        out_shape=jax.ShapeDtypeStruct((batch, seq, WA), F32),
        scratch_shapes=[pltpu.VMEM((seq, LANES), BF16), pltpu.VMEM((LANES, seq), BF16),
                        pltpu.VMEM((2 * TQ, LANES), BF16),
                        pltpu.VMEM((1, 2 * TQ), F32), pltpu.VMEM((1, 2 * TQ), F32),
                        pltpu.VMEM((LANES, 2 * TQ), F32),
                        pltpu.VMEM((TK, 2 * TQ), F32), pltpu.VMEM((TK, 2 * TQ), F32)],
        compiler_params=_cparams(("parallel", "parallel")),
        name="attn_prompt",
    )(flags, lamp, q3, kt_all, v3, bias)
    return out.reshape(batch * seq, WA)


def _attn_sample_kernel(pt_ref, lamp_ref, q_ref, kn_ref, vn_ref, bias_ref, *rest, n_pages, ts, lam_init):
    kp = rest[:n_pages]
    vp = rest[n_pages:2 * n_pages]
    o_ref = rest[2 * n_pages]
    s_ref = rest[2 * n_pages + 1]
    nr = HA * 2 * ts
    past = n_pages * PAGE_SIZE

    q = q_ref[...].astype(F32)
    qt = jnp.concatenate([q] * (2 * HA), axis=0)
    r = lax.broadcasted_iota(jnp.int32, (nr, WA), 0)
    c = lax.broadcasted_iota(jnp.int32, (nr, WA), 1)
    keep = lax.shift_right_logical(c, int(math.log2(DHA))) == lax.shift_right_logical(r, int(math.log2(ts)))
    qbd = jnp.where(keep, qt, 0.0).astype(BF16)
    zpad = jnp.zeros((PAGE_SIZE - ts, WA), F32)
    knp = jnp.concatenate([kn_ref[...], zpad], axis=0).astype(BF16)
    vnp = jnp.concatenate([vn_ref[...], zpad], axis=0).astype(BF16)

    for pg in range(n_pages):
        s_ref[:, pg * PAGE_SIZE:(pg + 1) * PAGE_SIZE] = _dot(qbd, kp[pg][...].astype(BF16))
    s_ref[:, past:past + PAGE_SIZE] = _dot_nt(qbd, knp)

    s = s_ref[...] + bias_ref[...]
    m = jnp.max(s, axis=-1, keepdims=True)
    p = jnp.exp2(s - m)
    l = jnp.sum(p, axis=-1, keepdims=True)
    pb = p.astype(BF16)
    lam = _lam(lamp_ref, lam_init)
    for h in range(HA):
        rs = slice(h * 2 * ts, (h + 1) * 2 * ts)
        cs = slice(h * LANES, (h + 1) * LANES)
        acc = _dot(pb[rs, past:past + PAGE_SIZE], vnp[:, cs])
        for pg in range(n_pages):
            vh = vp[pg][pl.ds(h, PAGE_SIZE, stride=HA), :].astype(BF16)
            acc = acc + _dot(pb[rs, pg * PAGE_SIZE:(pg + 1) * PAGE_SIZE], vh)
        acc = acc / l[rs]
        o_ref[:, cs] = acc[0:ts, :] - lam * acc[ts:2 * ts, :]


def _attn_sample(qa, ka, va, ckt, cv4, page_table, bias, lamp, l, lam_init, dec_batch, ts):
    n_pages = page_table.shape[1]
    nr = HA * 2 * ts
    width = n_pages * PAGE_SIZE + PAGE_SIZE
    tok = pl.BlockSpec((None, ts, WA), lambda b, pt: (b, 0, 0))
    page = lambda pg: pl.BlockSpec((None, None, HA * PAGE_SIZE, LANES), lambda b, pt: (l, pt[b, pg], 0, 0))
    grid_spec = pltpu.PrefetchScalarGridSpec(
        num_scalar_prefetch=1,
        grid=(dec_batch,),
        in_specs=[pl.BlockSpec((4, DHA), lambda b, pt: (0, 0)), tok, tok, tok,
                  pl.BlockSpec((nr, width), lambda b, pt: (0, 0))]
                 + [page(pg) for pg in range(n_pages)] * 2,
        out_specs=tok,
        scratch_shapes=[pltpu.VMEM((nr, width), F32)],
    )
    out = pl.pallas_call(
        functools.partial(_attn_sample_kernel, n_pages=n_pages, ts=ts, lam_init=lam_init),
        grid_spec=grid_spec,
        out_shape=jax.ShapeDtypeStruct((dec_batch, ts, WA), F32),
        compiler_params=_cparams(("parallel",)),
        name="attn_sample",
    )(page_table, lamp, qa.reshape(dec_batch, ts, WA), ka.reshape(dec_batch, ts, WA),
      va.reshape(dec_batch, ts, WA), bias, *([ckt] * n_pages), *([cv4] * n_pages))
    return out.reshape(dec_batch * ts, WA)


def _split3(x):
    x1 = x.astype(BF16)
    r1 = x - x1.astype(F32)
    x2 = r1.astype(BF16)
    x3 = (r1 - x2.astype(F32)).astype(BF16)
    return x1, x2, x3


def _gla_kernel(q_ref, k_ref, v_ref, g_ref, s0_ref, tril_ref, o_ref, sout_ref, st_scr, *, nb, c_in, c):
    ci = pl.program_id(1)
    units = [(b, pr) for b in range(nb) for pr in range(HG // 2)]

    @pl.when(ci == 0)
    def _():
        for b, pr in units:
            st_scr[b, pr] = s0_ref[b, pr].T

    def pad(x):
        if c_in == c:
            return x
        return jnp.concatenate([x, jnp.zeros((c - c_in, x.shape[1]), x.dtype)], axis=0)

    lane = lax.broadcasted_iota(jnp.int32, (1, LANES), 1)
    head0 = lane < DKG
    tt = lax.broadcasted_iota(jnp.int32, (2 * c, c), 0)
    ss = lax.broadcasted_iota(jnp.int32, (2 * c, c), 1)
    causal = jnp.bitwise_and(tt, c - 1) >= ss
    mid = c // 2 - 1

    def heads_on_rows(x):
        return jnp.concatenate([jnp.where(head0, x, 0.0), jnp.where(head0, 0.0, x)], axis=0).astype(BF16)

    g_all = jnp.concatenate([pad(g_ref[b]) for b in range(nb)], axis=1)
    tril = tril_ref[...]
    g3 = _split3(g_all)
    bc_all = _dot(tril, g3[0]) + _dot(tril, g3[1]) + _dot(tril, g3[2])

    chunk_range = jnp.max(-bc_all[c - 1:c, :])

    @pl.when(chunk_range <= GLA_RANGE_SAFE)
    def _():
        prep = []
        for b, pr in units:
            ks = slice(pr * LANES, (pr + 1) * LANES)
            q = pad(q_ref[b, :, ks])
            k = pad(k_ref[b, :, ks])
            bcum = bc_all[:, b * WGK + pr * LANES:b * WGK + (pr + 1) * LANES]
            bm = bcum[mid:mid + 1, :]
            bl = bcum[c - 1:c, :]
            prep.append(dict(
                qt=heads_on_rows(q * jnp.exp(bcum - bm)),
                qi=heads_on_rows(q * jnp.exp(bcum)),
                kt=(k * jnp.exp(bm - bcum)).astype(BF16),
                kd=(k * jnp.exp(bl - bcum)).astype(BF16),
                decay=jnp.exp(bl),
                v2=pad(v_ref[b, :, 2 * pr * DVG:2 * (pr + 1) * DVG]).astype(BF16)))

        amat = [jnp.where(causal, _dot_nt(u["qt"], u["kt"]), 0.0).astype(BF16) for u in prep]
        inter = [_dot_nt(u["qi"], st_scr[b, pr].astype(BF16)) for u, (b, pr) in zip(prep, units)]
        for u, a, it, (b, pr) in zip(prep, amat, inter, units):
            for hh in range(2):
                o = (_dot(a[hh * c:(hh + 1) * c, :], u["v2"][:, hh * DVG:(hh + 1) * DVG])
                     + it[hh * c:(hh + 1) * c, :])
                o_ref[b, :, (2 * pr + hh) * DVG:(2 * pr + hh + 1) * DVG] = o[0:c_in, :]
        for u, (b, pr) in zip(prep, units):
            upd = _dot_tn(u["v2"], u["kd"])
            st_scr[b, pr] = st_scr[b, pr] * u["decay"] + jnp.where(head0, upd[0:DVG, :], upd[DVG:2 * DVG, :])

    @pl.when(jnp.logical_not(chunk_range <= GLA_RANGE_SAFE))
    def _():
        row_head0 = lax.broadcasted_iota(jnp.int32, (LANES, 1), 0) < DKG

        def columns(x):
            return jnp.concatenate([x, jnp.zeros((LANES - c_in, LANES), F32)], axis=0).T

        def unit_body(u, carry):
            b = u // (HG // 2)
            pr = u % (HG // 2)
            ks = pl.ds(pl.multiple_of(pr * LANES, LANES), LANES)
            qc = columns(q_ref[b, :, ks])
            kc = columns(k_ref[b, :, ks])
            ac = columns(jnp.exp(g_ref[b, :, ks]))
            v2 = v_ref[b, :, pl.ds(pl.multiple_of(pr * 2 * DVG, 2 * DVG), 2 * DVG)]
            s = st_scr[b, pr].T
            for t in range(c_in):
                vrow = jnp.where(row_head0, v2[t:t + 1, 0:DVG], v2[t:t + 1, DVG:2 * DVG])
                s = s * ac[:, t:t + 1] + kc[:, t:t + 1] * vrow
                w = qc[:, t:t + 1] * s
                o_ref[b, pl.ds(t, 1), pl.ds(pl.multiple_of(pr * 2 * DVG, 2 * DVG), DVG)] = (
                    jnp.sum(w[0:DKG, :], axis=0, keepdims=True))
                o_ref[b, pl.ds(t, 1), pl.ds(pl.multiple_of(pr * 2 * DVG + DVG, DVG), DVG)] = (
                    jnp.sum(w[DKG:2 * DKG, :], axis=0, keepdims=True))
            st_scr[b, pr] = s.T
            return carry

        lax.fori_loop(0, len(units), unit_body, 0)

    @pl.when(ci == pl.num_programs(1) - 1)
    def _():
        for b, pr in units:
            sout_ref[b, pr] = st_scr[b, pr].T


def _gla(qg, kg, vg, la, s0, s0_layer, batch, seq, nb, c_in, c):
    n_chunks = seq // c_in
    tril = jnp.tril(jnp.ones((c, c), BF16))
    tok = lambda w: pl.BlockSpec((nb, c_in, w), lambda bi, ci: (bi, ci, 0))
    st = pl.BlockSpec((nb, HG // 2, 2 * DKG, DVG), lambda bi, ci: (bi, 0, 0, 0))
    st_in = pl.BlockSpec((None, nb, HG // 2, 2 * DKG, DVG), lambda bi, ci: (s0_layer, bi, 0, 0, 0))
    o, s_new = pl.pallas_call(
        functools.partial(_gla_kernel, nb=nb, c_in=c_in, c=c),
        grid=(batch // nb, n_chunks),
        in_specs=[tok(WGK), tok(WGK), tok(WGV), tok(WGK), st_in,
                  pl.BlockSpec((c, c), lambda bi, ci: (0, 0))],
        out_specs=[tok(WGV), st],
        out_shape=[jax.ShapeDtypeStruct((batch, seq, WGV), F32),
                   jax.ShapeDtypeStruct((batch, HG // 2, 2 * DKG, DVG), F32)],
        scratch_shapes=[pltpu.VMEM((nb, HG // 2, 2 * DKG, DVG), F32)],
        compiler_params=_cparams(("parallel", "arbitrary")),
        name="gla",
    )(qg.reshape(batch, seq, WGK), kg.reshape(batch, seq, WGK), vg.reshape(batch, seq, WGV),
      la.reshape(batch, seq, WGK), s0, tril)
    return o.reshape(batch * seq, WGV), s_new.reshape(batch, HG, DKG, DVG)


def _merge_kernel(x_ref, oa_ref, og_ref, rg_ref, gmix_ref, subln_ref, glan_ref,
                  wgate_ref, wdo_ref, wgo_ref, wout_ref, o_ref, ya_ref, yg_ref, *, lam_init):
    x = x_ref[...]
    h = _rms(x, gmix_ref[...]).astype(BF16)
    gate = jax.nn.sigmoid(_dot(h, wgate_ref[...]))
    for hd in range(HA):
        cs = slice(hd * LANES, (hd + 1) * LANES)
        ya_ref[:, cs] = (_rms(oa_ref[:, cs], subln_ref[...]) * (1.0 - lam_init)).astype(BF16)
        yg_ref[:, cs] = (_rms(og_ref[:, cs], glan_ref[...]) * jax.nn.silu(rg_ref[:, cs])).astype(BF16)
    ya = _dot(ya_ref[...], wdo_ref[...])
    yg = _dot(yg_ref[...], wgo_ref[...])
    mix = gate[:, 0:D_MODEL] * ya + gate[:, D_MODEL:2 * D_MODEL] * yg
    o_ref[...] = x + _dot(mix.astype(BF16), wout_ref[...])


def _merge(x, oa, og, rg, l, p, lam_init):
    t = x.shape[0]
    row = lambda w: pl.BlockSpec((TM, w), lambda i: (i, 0))
    lay2 = lambda w: pl.BlockSpec((None, 1, w), lambda i: (l, 0, 0))
    lay3 = lambda a, b: pl.BlockSpec((None, a, b), lambda i: (l, 0, 0), pipeline_mode=pl.Buffered(1))
    return pl.pallas_call(
        functools.partial(_merge_kernel, lam_init=lam_init),
        grid=(t // TM,),
        in_specs=[row(D_MODEL), row(WA), row(WGV), row(WGV), lay2(D_MODEL), lay2(2 * DHA), lay2(DVG),
                  lay3(D_MODEL, 2 * D_MODEL), lay3(WA, D_MODEL), lay3(WGV, D_MODEL), lay3(D_MODEL, D_MODEL)],
        out_specs=row(D_MODEL),
        out_shape=jax.ShapeDtypeStruct((t, D_MODEL), F32),
        scratch_shapes=[pltpu.VMEM((TM, WA), BF16), pltpu.VMEM((TM, WGV), BF16)],
        compiler_params=_cparams(("parallel",)),
        name="merge",
    )(x, oa, og, rg, p["norm_mix"], p["subln"], p["gla_norm"],
      p["w_gate"], p["w_diff_out"], p["w_gla_out"], p["w_out"])


def _ffn_kernel(x_ref, g_ref, wg_ref, wu_ref, wd_ref, o_ref, *, chunk):
    x = x_ref[...]
    h = _rms(x, g_ref[...]).astype(BF16)
    acc = x
    for c0 in range(0, wg_ref.shape[1], chunk):
        a = jax.nn.silu(_dot(h, wg_ref[:, c0:c0 + chunk])) * _dot(h, wu_ref[:, c0:c0 + chunk])
        acc = acc + _dot(a.astype(BF16), wd_ref[c0:c0 + chunk, :])
    o_ref[...] = acc


def _ffn(x, l, p):
    t = x.shape[0]
    f = p["w_ffn_gate"].shape[2]
    row = pl.BlockSpec((TM, D_MODEL), lambda i: (i, 0))
    once = pl.Buffered(1)
    return pl.pallas_call(
        functools.partial(_ffn_kernel, chunk=f // 2),
        grid=(t // TM,),
        in_specs=[row, pl.BlockSpec((None, 1, D_MODEL), lambda i: (l, 0, 0)),
                  pl.BlockSpec((None, D_MODEL, f), lambda i: (l, 0, 0), pipeline_mode=once),
                  pl.BlockSpec((None, D_MODEL, f), lambda i: (l, 0, 0), pipeline_mode=once),
                  pl.BlockSpec((None, f, D_MODEL), lambda i: (l, 0, 0), pipeline_mode=once)],
        out_specs=row,
        out_shape=jax.ShapeDtypeStruct((t, D_MODEL), F32),
        compiler_params=_cparams(("parallel",)),
        name="ffn",
    )(x, p["norm_ffn"], p["w_ffn_gate"], p["w_ffn_up"], p["w_ffn_down"])


def kernel(x_prompt, x_sample, cache_k, cache_v, state_gla, page_table, rel_bias, norm_mix, w_in, w_alpha2, b_alpha, qn_gain, kn_gain, lam_q1, lam_k1, lam_q2, lam_k2, subln_gain, gla_norm_gain, w_diff_out, w_gla_out, w_out, norm_ffn, w_ffn_gate, w_ffn_up, w_ffn_down):
    batch, seq, _ = x_prompt.shape
    dec_batch, ts, _ = x_sample.shape
    n_pages = page_table.shape[1]
    n_pool = cache_k.shape[1]
    past = n_pages * PAGE_SIZE
    depth = w_in.shape[0]
    gate0 = MAIN_W + GATE_RANK

    vec = lambda a: a.reshape(depth, 1, a.shape[-1])
    p = {
        "norm_mix": vec(norm_mix), "norm_ffn": vec(norm_ffn), "b_alpha": vec(b_alpha),
        "gq": vec(jnp.tile(qn_gain, (1, WA // DHA))), "gk": vec(jnp.tile(kn_gain, (1, WA // DHA))),
        "subln": vec(subln_gain), "gla_norm": vec(gla_norm_gain),
        "w_main": w_in[:, :, :MAIN_W].astype(BF16),
        "w_alr": jnp.pad(w_in[:, :, MAIN_W:gate0], ((0, 0), (0, 0), (0, LANES - GATE_RANK))).astype(BF16),
        "w_gate": w_in[:, :, gate0:].astype(BF16),
        "w_alpha2": jnp.pad(w_alpha2, ((0, 0), (0, LANES - GATE_RANK), (0, 0))).astype(BF16),
        "w_diff_out": w_diff_out.astype(BF16), "w_gla_out": w_gla_out.astype(BF16), "w_out": w_out.astype(BF16),
        "w_ffn_gate": w_ffn_gate.astype(BF16), "w_ffn_up": w_ffn_up.astype(BF16),
        "w_ffn_down": w_ffn_down.astype(BF16),
        "gsum": jnp.kron(jnp.eye(WA // DHA, dtype=F32), jnp.full((DHA, DHA), 1.0 / DHA, F32)).astype(BF16),
    }
    lamp = jnp.stack([lam_q1, lam_k1, lam_q2, lam_k2], axis=1)

    ckt = jnp.transpose(cache_k, (0, 1, 3, 4, 5, 2)).reshape(depth, n_pool, WA, PAGE_SIZE)
    cv4 = cache_v.reshape(depth, n_pool, PAGE_SIZE * HA, LANES)
    sg = state_gla.reshape(depth, dec_batch, HG // 2, 2 * DKG, DVG)
    s0p = jnp.zeros((1, batch, HG // 2, 2 * DKG, DVG), F32)

    bias_p = _bias_tables(rel_bias, 3, TK, 2 * TQ, TQ, 0, TK, True, True)
    bias_s = _bias_tables(rel_bias, 1, 2 * ts, past + PAGE_SIZE, ts, past, 0, False, False)
    bias_s = bias_s.reshape(HA * 2 * ts, past + PAGE_SIZE)
    flags = _logit_bound_flags(qn_gain, kn_gain, rel_bias)

    xp = x_prompt.reshape(batch * seq, D_MODEL)
    xs = x_sample.reshape(dec_batch * ts, D_MODEL)
    kv = ()
    sp_l, ks_l, vs_l, ss_l = [], [], [], []
    for l in range(depth):
        lam_init = 0.8 - 0.6 * math.exp(-0.3 * l)
        qa, kt_all, v4_all, qg, kg, vg, rg, la = _inproj(xp, l, p, kv, batch, seq)
        kv = (kt_all, v4_all)
        oa = _attn_prompt(qa, kt_all, v4_all, bias_p, flags, lamp[l], l, lam_init, batch, seq)
        og, sp = _gla(qg, kg, vg, la, s0p, 0, batch, seq, batch, GLA_C, GLA_C)
        xp = _merge(xp, oa, og, rg, l, p, lam_init)
        xp = _ffn(xp, l, p)
        sp_l.append(sp)
        qa, ka, va, qg, kg, vg, rg, la = _inproj(xs, l, p)
        oa = _attn_sample(qa, ka, va, ckt, cv4, page_table, bias_s, lamp[l], l, lam_init, dec_batch, ts)
        og, ss = _gla(qg, kg, vg, la, sg, l, dec_batch, ts, GLA_SB, ts, 2 * ts)
        xs = _merge(xs, oa, og, rg, l, p, lam_init)
        xs = _ffn(xs, l, p)
        ks_l.append(ka)
        vs_l.append(va)
        ss_l.append(ss)
    kt_all, v4_all = kv
    k_prompt = jnp.transpose(kt_all.reshape(depth, batch, HA, 2, DHA, seq), (0, 1, 5, 2, 3, 4))
    v_prompt = v4_all.reshape(depth, batch, seq, HA, 2 * DHA)
    return (xp.reshape(batch, seq, D_MODEL), xs.reshape(dec_batch, ts, D_MODEL),
            k_prompt, v_prompt, jnp.stack(sp_l),
            jnp.stack(ks_l).reshape(depth, dec_batch, ts, HA, 2, DHA),
            jnp.stack(vs_l).reshape(depth, dec_batch, ts, HA, 2 * DHA),
            jnp.stack(ss_l))
```

```python
import functools
import math

import jax
import jax.numpy as jnp
from jax import lax
from jax.experimental import pallas as pl
from jax.experimental.pallas import tpu as pltpu

F32 = jnp.float32
BF16 = jnp.bfloat16

D_MODEL = 1024
HA = 4
DHA = 64
HG = 4
DKG = 64
DVG = 128
GATE_RANK = 16
GATE_NORM = 16.0
N_BUCKETS = 32
MAX_DISTANCE = 128
PAGE_SIZE = 128
EPS = 1e-6
WA = HA * 2 * DHA
WGK = HG * DKG
WGV = HG * DVG
MAIN_W = 3 * WA + 2 * WGK + 2 * WGV
LANES = 128
NEG = -1e30
LOG2E = math.log2(math.e)
GLA_RANGE_SAFE = 80.0
LOGIT_SAFE = 60.0

TM = 512
TQ = 512
TK = 512
GLA_C = 64
GLA_SB = 8
VMEM_LIMIT = 56 * 1024 * 1024


def _cparams(sem):
    return pltpu.CompilerParams(dimension_semantics=sem, vmem_limit_bytes=VMEM_LIMIT)


def _rms(x, gain):
    ms = jnp.mean(x * x, axis=-1, keepdims=True)
    return x * lax.rsqrt(ms + EPS) * gain


def _dot(a, b):
    return jnp.dot(a, b, preferred_element_type=F32)


def _dot_nt(a, b):
    return lax.dot_general(a, b, (((1,), (1,)), ((), ())), preferred_element_type=F32)


def _dot_tn(a, b):
    return lax.dot_general(a, b, (((0,), (0,)), ((), ())), preferred_element_type=F32)


def _lam(lamp_ref, lam_init):
    a = jnp.sum(lamp_ref[0:1, :] * lamp_ref[1:2, :], axis=-1, keepdims=True)
    b = jnp.sum(lamp_ref[2:3, :] * lamp_ref[3:4, :], axis=-1, keepdims=True)
    return jnp.exp(a) - jnp.exp(b) + lam_init


def _bias_kernel(rb_ref, o_ref, *, period, off0, off_step, transposed, shifted):
    h = pl.program_id(0)
    t = pl.program_id(1)
    rows, cols = o_ref.shape
    r = lax.broadcasted_iota(jnp.int32, (rows, cols), 0)
    c = lax.broadcasted_iota(jnp.int32, (rows, cols), 1)
    if transposed:
        r, c = c, r
    d = off0 + t * off_step + jnp.bitwise_and(r, period - 1) - c
    n = jnp.maximum(d, 0)
    max_exact = N_BUCKETS // 2
    nf = jnp.maximum(n, 1).astype(F32)
    large = max_exact + (jnp.log(nf / max_exact) / math.log(MAX_DISTANCE / max_exact)
                         * (N_BUCKETS - max_exact)).astype(jnp.int32)
    large = jnp.minimum(large, N_BUCKETS - 1)
    bucket = jnp.where(n < max_exact, n, large)
    val = jnp.zeros((rows, cols), F32)
    for k in range(N_BUCKETS):
        val = jnp.where(bucket == k, rb_ref[k, h], val)
    if shifted:
        val = val - rb_ref[N_BUCKETS - 1, h]
    o_ref[...] = jnp.where(d >= 0, val * LOG2E, NEG)


def _bias_tables(rel_bias, n_t, rows, cols, period, off0, off_step, transposed, shifted):
    return pl.pallas_call(
        functools.partial(_bias_kernel, period=period, off0=off0, off_step=off_step,
                          transposed=transposed, shifted=shifted),
        grid=(HA, n_t),
        in_specs=[pl.BlockSpec(memory_space=pltpu.SMEM)],
        out_specs=pl.BlockSpec((None, None, rows, cols), lambda h, t: (h, t, 0, 0)),
        out_shape=jax.ShapeDtypeStruct((HA, n_t, rows, cols), F32),
        compiler_params=_cparams(("arbitrary", "arbitrary")),
        name="bias_tables",
    )(rel_bias)


def _inproj_kernel(*refs, prompt, n_alias, ts):
    (x_ref, gmix_ref, w_ref, walr_ref, wa2_ref, ba_ref, gq_ref, gk_ref, gsum_ref) = refs[:9]
    outs = refs[9 + n_alias:]
    if prompt:
        (qa_ref, ka_ref, va_ref, qg_ref, kg_ref, vg_ref, rg_ref, la_ref) = outs
    else:
        (qa_ref, ka_ref, va_ref, kst_ref, vs4_ref, qg_ref, kg_ref, vg_ref, rg_ref, la_ref, kscr_ref) = outs
    h = _rms(x_ref[...], gmix_ref[...]).astype(BF16)
    proj = _dot(h, w_ref[...])
    gs = gsum_ref[...]
    qa = proj[:, 0:WA]
    ka = proj[:, WA:2 * WA]
    qms = _dot((qa * qa).astype(BF16), gs)
    kms = _dot((ka * ka).astype(BF16), gs)
    qa_ref[...] = (qa * lax.rsqrt(qms + EPS) * gq_ref[...] * (DHA ** -0.5 * LOG2E)).astype(BF16)
    kan = ka * lax.rsqrt(kms + EPS) * gk_ref[...]
    if prompt:
        ka_ref[...] = kan.T
        for hd in range(HA):
            va_ref[pl.ds(hd, TM, stride=HA), :] = proj[:, 2 * WA + hd * LANES:2 * WA + (hd + 1) * LANES]
    else:
        ka_ref[...] = kan
        va_ref[...] = proj[:, 2 * WA:3 * WA]
        for hd in range(HA):
            vs4_ref[pl.ds(hd, TM, stride=HA), :] = proj[:, 2 * WA + hd * LANES:2 * WA + (hd + 1) * LANES]
        for j in range(WA // LANES):
            kscr_ref[j] = kan[:, j * LANES:(j + 1) * LANES]
        nbt = TM // ts
        for tile in range(kst_ref.shape[2] // nbt):
            @pl.when(pl.program_id(0) == tile)
            def _(tile=tile):
                lo, hi = tile * nbt, (tile + 1) * nbt
                for t in range(ts):
                    for j in range(WA // LANES):
                        rows = kscr_ref[j, pl.ds(t, nbt, stride=ts), :]
                        parts = [jnp.zeros((n, LANES), F32) for n in (lo,) if n] + [rows]
                        parts += [jnp.zeros((n, LANES), F32) for n in (LANES - hi,) if n]
                        tr = jnp.concatenate(parts, axis=0).T
                        kst_ref[t, j * LANES:(j + 1) * LANES, lo:hi] = tr[:, lo:hi]
    o = 3 * WA
    qg_ref[...] = proj[:, o:o + WGK] * (DKG ** -0.5)
    kg_ref[...] = proj[:, o + WGK:o + 2 * WGK]
    o += 2 * WGK
    vg_ref[...] = proj[:, o:o + WGV]
    rg_ref[...] = proj[:, o + WGV:o + 2 * WGV]
    alr = _dot(h, walr_ref[...])
    z = _dot(alr.astype(BF16), wa2_ref[...]) + ba_ref[...]
    log_sig = jnp.minimum(z, 0.0) - jnp.log(1.0 + jnp.exp(-jnp.abs(z)))
    la_ref[...] = log_sig * (1.0 / GATE_NORM)


def _inproj(x, l, p, prompt, kv_prev, batch, seq):
    t = x.shape[0]
    depth = p["w_main"].shape[0]
    row = lambda w: pl.BlockSpec((TM, w), lambda i: (i, 0))
    lay2 = lambda w: pl.BlockSpec((None, 1, w), lambda i: (l, 0, 0))
    lay3 = lambda a, b: pl.BlockSpec((None, a, b), lambda i: (l, 0, 0), pipeline_mode=pl.Buffered(1))
    in_specs = [row(D_MODEL), lay2(D_MODEL), lay3(D_MODEL, MAIN_W), lay3(D_MODEL, LANES),
                lay3(LANES, WGK), lay2(WGK), lay2(WA), lay2(WA),
                pl.BlockSpec((WA, WA), lambda i: (0, 0))]
    args = [x, p["norm_mix"], p["w_main"], p["w_alr"], p["w_alpha2"], p["b_alpha"], p["gq"], p["gk"], p["gsum"]]
    rest = [(WGK, F32), (WGK, F32), (WGV, F32), (WGV, F32), (WGK, F32)]
    if prompt:
        spb = seq // TM
        kv_specs = [pl.BlockSpec((None, None, WA, TM), lambda i: (l, i // spb, 0, i % spb)),
                    pl.BlockSpec((None, TM * HA, LANES), lambda i: (l, i, 0))]
        kv_shapes = [jax.ShapeDtypeStruct((depth, batch, WA, seq), F32),
                     jax.ShapeDtypeStruct((depth, t * HA, LANES), F32)]
        first_kv = 1
        scratch = []
    else:
        kv_specs = [row(WA), row(WA),
                    pl.BlockSpec((None, seq, WA, batch), lambda i: (l, 0, 0, 0)),
                    pl.BlockSpec((None, TM * HA, LANES), lambda i: (l, i, 0))]
        kv_shapes = [jax.ShapeDtypeStruct((t, WA), F32)] * 2 + [
            jax.ShapeDtypeStruct((depth, seq, WA, batch), F32),
            jax.ShapeDtypeStruct((depth, t * HA, LANES), F32)]
        first_kv = 3
        scratch = [pltpu.VMEM((WA // LANES, TM, LANES), F32)]
    aliases = {}
    for n, a in enumerate(kv_prev):
        in_specs.append(pl.BlockSpec(memory_space=pl.ANY))
        args.append(a)
        aliases[9 + n] = first_kv + n
    return pl.pallas_call(
        functools.partial(_inproj_kernel, prompt=prompt, n_alias=len(aliases), ts=seq),
        grid=(t // TM,),
        in_specs=in_specs,
        out_specs=[row(WA)] + kv_specs + [row(w) for w, _ in rest],
        out_shape=[jax.ShapeDtypeStruct((t, WA), BF16)] + kv_shapes
                  + [jax.ShapeDtypeStruct((t, w), dt) for w, dt in rest],
        scratch_shapes=scratch,
        input_output_aliases=aliases,
        compiler_params=_cparams(("parallel",) if prompt else ("arbitrary",)),
        name="inproj",
    )(*args)


def _bound_kernel(gq_ref, gk_ref, rb_ref, o_ref):
    mq = jnp.max(jnp.abs(gq_ref[...]), axis=1, keepdims=True)
    mk = jnp.max(jnp.abs(gk_ref[...]), axis=1, keepdims=True)
    rb = rb_ref[...]
    shifted = jnp.abs(rb - rb[N_BUCKETS - 1:N_BUCKETS, :])
    bmax = jnp.max(jnp.max(shifted, axis=1, keepdims=True), axis=0, keepdims=True)
    bound = LOG2E * (1.01 * DHA ** 0.5 * mq * mk + bmax)
    o_ref[...] = jnp.broadcast_to((bound <= LOGIT_SAFE).astype(jnp.int32), o_ref.shape)


def _logit_bound_flags(qn_gain, kn_gain, rel_bias):
    depth = qn_gain.shape[0]
    return pl.pallas_call(
        _bound_kernel,
        out_shape=jax.ShapeDtypeStruct((depth, LANES), jnp.int32),
        name="logit_bound",
    )(qn_gain, kn_gain, rel_bias)


def _attn_kernel(flag_ref, lamp_ref, q_ref, kt_ref, v_ref, bias_ref, o_ref,
                 kb_ref, vtb_ref, qs_ref, m_ref, l_ref, acc_ref, sa_ref, sb_ref, *, lam_init, seq, layer):
    hd = pl.program_id(1)
    nq = seq // TQ
    bounded = flag_ref[layer, 0] == 1

    for c0 in range(0, seq, TK):
        kb_ref[c0:c0 + TK, :] = kt_ref[:, c0:c0 + TK].T.astype(BF16)
        vh = v_ref[pl.ds(c0 * HA + hd, TK, stride=HA), :]
        vtb_ref[:, c0:c0 + TK] = vh.T.astype(BF16)

    lane = lax.broadcasted_iota(jnp.int32, (1, LANES), 1)
    lam = _lam(lamp_ref, lam_init)

    def logits(j, dst_ref):
        start = pl.multiple_of(j * TK, TK)
        dst_ref[...] = _dot_nt(kb_ref[pl.ds(start, TK), :], qs_ref[...])

    def start_block(i):
        q = q_ref[pl.ds(pl.multiple_of(i * TQ, TQ), TQ), :]
        qs_ref[0:TQ, :] = jnp.where(lane < DHA, q, jnp.zeros_like(q))
        qs_ref[TQ:2 * TQ, :] = jnp.where(lane >= DHA, q, jnp.zeros_like(q))
        m_ref[...] = jnp.full(m_ref.shape, NEG, F32)
        l_ref[...] = jnp.zeros(l_ref.shape, F32)
        acc_ref[...] = jnp.zeros(acc_ref.shape, F32)
        logits(jnp.where(bounded, i, 0), sa_ref)

    start_block(0)


    def biased(src_ref, table):
        b = bias_ref[table]
        return jnp.concatenate([src_ref[:, 0:TQ] + b, src_ref[:, TQ:2 * TQ] + b], axis=1)

    def accumulate(p, j):
        start = pl.multiple_of(j * TK, TK)
        l_ref[...] += jnp.sum(p, axis=0, keepdims=True)
        acc_ref[...] += _dot(vtb_ref[:, pl.ds(start, TK)], p.astype(BF16))

    def q_block(i, carry):
        jl = i

        @pl.when(bounded)
        def _():
            @pl.when(jl == 0)
            def _():
                accumulate(jnp.exp2(biased(sa_ref, 0)), jl)

            @pl.when(jl >= 1)
            def _():
                logits(jl - 1, sb_ref)
                accumulate(jnp.exp2(biased(sa_ref, 0)), jl)
                logits(0, sa_ref)
                accumulate(jnp.exp2(biased(sb_ref, 1)), jl - 1)

            n_far = jnp.maximum(jl - 1, 0)

            def far_pair(jj, carry):
                j0 = 2 * jj
                logits(jnp.minimum(j0 + 1, jl), sb_ref)
                accumulate(jnp.exp2(sa_ref[...]), j0)

                @pl.when(j0 + 1 < n_far)
                def _():
                    logits(jnp.minimum(j0 + 2, jl), sa_ref)
                    accumulate(jnp.exp2(sb_ref[...]), j0 + 1)

                return carry

            lax.fori_loop(0, (n_far + 1) // 2, far_pair, 0)

        @pl.when(jnp.logical_not(bounded))
        def _():
            def update(src_ref, j):
                table = jnp.minimum(jl - j, 2)
                start = pl.multiple_of(j * TK, TK)
                s = biased(src_ref, table)
                m_prev = m_ref[...]
                m_new = jnp.maximum(m_prev, jnp.max(s, axis=0, keepdims=True))
                alpha = jnp.exp2(m_prev - m_new)
                p = jnp.exp2(s - m_new)
                l_ref[...] = alpha * l_ref[...] + jnp.sum(p, axis=0, keepdims=True)
                acc_ref[...] = alpha * acc_ref[...] + _dot(vtb_ref[:, pl.ds(start, TK)], p.astype(BF16))
                m_ref[...] = m_new

            def pair_body(jj, carry):
                j0 = 2 * jj
                logits(jnp.minimum(j0 + 1, jl), sb_ref)
                update(sa_ref, j0)

                @pl.when(j0 + 1 <= jl)
                def _():
                    logits(jnp.minimum(j0 + 2, jl), sa_ref)
                    update(sb_ref, j0 + 1)

                return carry

            lax.fori_loop(0, jl // 2 + 1, pair_body, 0)

        o = acc_ref[...] * (1.0 / l_ref[...])
        o_ref[pl.ds(pl.multiple_of(i * TQ, TQ), TQ), :] = (o[:, 0:TQ] - lam * o[:, TQ:2 * TQ]).T
        start_block(jnp.minimum(i + 1, nq - 1))
        return carry

    lax.fori_loop(0, nq, q_block, 0)


def _attn_prompt(qa, kt_all, v4_all, bias, flags, lamp, l, lam_init, batch, seq):
    depth = kt_all.shape[0]
    q3 = qa.reshape(batch, seq, WA)
    v3 = v4_all.reshape(depth * batch, seq * HA, LANES)
    blk = pl.BlockSpec((None, seq, LANES), lambda b, h: (b, 0, h))
    out = pl.pallas_call(
        functools.partial(_attn_kernel, lam_init=lam_init, seq=seq, layer=l),
        grid=(batch, HA),
        in_specs=[pl.BlockSpec(memory_space=pltpu.SMEM),
                  pl.BlockSpec((4, DHA), lambda b, h: (0, 0)),
                  blk,
                  pl.BlockSpec((None, None, LANES, seq), lambda b, h: (l, b, h, 0)),
                  pl.BlockSpec((None, seq * HA, LANES), lambda b, h: (l * batch + b, 0, 0)),
                  pl.BlockSpec((None, 3, TK, TQ), lambda b, h: (h, 0, 0, 0))],
        out_specs=blk,
        out_shape=jax.ShapeDtypeStruct((batch, seq, WA), F32),
        scratch_shapes=[pltpu.VMEM((seq, LANES), BF16), pltpu.VMEM((LANES, seq), BF16),
                        pltpu.VMEM((2 * TQ, LANES), BF16),
                        pltpu.VMEM((1, 2 * TQ), F32), pltpu.VMEM((1, 2 * TQ), F32),
                        pltpu.VMEM((LANES, 2 * TQ), F32),
                        pltpu.VMEM((TK, 2 * TQ), F32), pltpu.VMEM((TK, 2 * TQ), F32)],
        compiler_params=_cparams(("parallel", "parallel")),
        name="attn_prompt",
    )(flags, lamp, q3, kt_all, v3, bias)
    return out.reshape(batch * seq, WA)


def _attn_sample_kernel(pt_ref, lamp_ref, q_ref, kn_ref, vn_ref, bias_ref, *rest, n_pages, ts, lam_init):
    kp = rest[:n_pages]
    vp = rest[n_pages:2 * n_pages]
    o_ref = rest[2 * n_pages]
    s_ref = rest[2 * n_pages + 1]
    nr = HA * 2 * ts
    past = n_pages * PAGE_SIZE

    q = q_ref[...].astype(F32)
    qt = jnp.concatenate([q] * (2 * HA), axis=0)
    r = lax.broadcasted_iota(jnp.int32, (nr, WA), 0)
    c = lax.broadcasted_iota(jnp.int32, (nr, WA), 1)
    keep = lax.shift_right_logical(c, int(math.log2(DHA))) == lax.shift_right_logical(r, int(math.log2(ts)))
    qbd = jnp.where(keep, qt, 0.0).astype(BF16)
    zpad = jnp.zeros((PAGE_SIZE - ts, WA), F32)
    knp = jnp.concatenate([kn_ref[...], zpad], axis=0).astype(BF16)
    vnp = jnp.concatenate([vn_ref[...], zpad], axis=0).astype(BF16)

    for pg in range(n_pages):
        s_ref[:, pg * PAGE_SIZE:(pg + 1) * PAGE_SIZE] = _dot(qbd, kp[pg][...].astype(BF16))
    s_ref[:, past:past + PAGE_SIZE] = _dot_nt(qbd, knp)

    s = s_ref[...] + bias_ref[...]
    m = jnp.max(s, axis=-1, keepdims=True)
    p = jnp.exp2(s - m)
    l = jnp.sum(p, axis=-1, keepdims=True)
    pb = p.astype(BF16)
    lam = _lam(lamp_ref, lam_init)
    for h in range(HA):
        rs = slice(h * 2 * ts, (h + 1) * 2 * ts)
        cs = slice(h * LANES, (h + 1) * LANES)
        acc = _dot(pb[rs, past:past + PAGE_SIZE], vnp[:, cs])
        for pg in range(n_pages):
            vh = vp[pg][pl.ds(h, PAGE_SIZE, stride=HA), :].astype(BF16)
            acc = acc + _dot(pb[rs, pg * PAGE_SIZE:(pg + 1) * PAGE_SIZE], vh)
        acc = acc / l[rs]
        o_ref[:, cs] = acc[0:ts, :] - lam * acc[ts:2 * ts, :]


def _attn_sample(qa, ka, va, ckt, cv4, page_table, bias, lamp, l, lam_init, dec_batch, ts):
    n_pages = page_table.shape[1]
    nr = HA * 2 * ts
    width = n_pages * PAGE_SIZE + PAGE_SIZE
    tok = pl.BlockSpec((None, ts, WA), lambda b, pt: (b, 0, 0))
    page = lambda pg: pl.BlockSpec((None, None, HA * PAGE_SIZE, LANES), lambda b, pt: (l, pt[b, pg], 0, 0))
    grid_spec = pltpu.PrefetchScalarGridSpec(
        num_scalar_prefetch=1,
        grid=(dec_batch,),
        in_specs=[pl.BlockSpec((4, DHA), lambda b, pt: (0, 0)), tok, tok, tok,
                  pl.BlockSpec((nr, width), lambda b, pt: (0, 0))]
                 + [page(pg) for pg in range(n_pages)] * 2,
        out_specs=tok,
        scratch_shapes=[pltpu.VMEM((nr, width), F32)],
    )
    out = pl.pallas_call(
        functools.partial(_attn_sample_kernel, n_pages=n_pages, ts=ts, lam_init=lam_init),
        grid_spec=grid_spec,
        out_shape=jax.ShapeDtypeStruct((dec_batch, ts, WA), F32),
        compiler_params=_cparams(("parallel",)),
        name="attn_sample",
    )(page_table, lamp, qa.reshape(dec_batch, ts, WA), ka.reshape(dec_batch, ts, WA),
      va.reshape(dec_batch, ts, WA), bias, *([ckt] * n_pages), *([cv4] * n_pages))
    return out.reshape(dec_batch * ts, WA)


def _split3(x):
    x1 = x.astype(BF16)
    r1 = x - x1.astype(F32)
    x2 = r1.astype(BF16)
    x3 = (r1 - x2.astype(F32)).astype(BF16)
    return x1, x2, x3


def _gla_kernel(q_ref, k_ref, v_ref, g_ref, s0_ref, tril_ref, *rest, nb, c_in, c):
    o_ref, sout_ref, st_scr = rest[-3:]
    ci = pl.program_id(1)
    units = [(b, pr) for b in range(nb) for pr in range(HG // 2)]

    @pl.when(ci == 0)
    def _():
        for b, pr in units:
            st_scr[b, pr] = s0_ref[b, pr].T

    def pad(x):
        if c_in == c:
            return x
        return jnp.concatenate([x, jnp.zeros((c - c_in, x.shape[1]), x.dtype)], axis=0)

    lane = lax.broadcasted_iota(jnp.int32, (1, LANES), 1)
    head0 = lane < DKG
    tt = lax.broadcasted_iota(jnp.int32, (2 * c, c), 0)
    ss = lax.broadcasted_iota(jnp.int32, (2 * c, c), 1)
    causal = jnp.bitwise_and(tt, c - 1) >= ss
    mid = c // 2 - 1

    def heads_on_rows(x):
        return jnp.concatenate([jnp.where(head0, x, 0.0), jnp.where(head0, 0.0, x)], axis=0).astype(BF16)

    g_all = jnp.concatenate([pad(g_ref[b]) for b in range(nb)], axis=1)
    tril = tril_ref[...]
    g3 = _split3(g_all)
    bc_all = _dot(tril, g3[0]) + _dot(tril, g3[1]) + _dot(tril, g3[2])

    chunk_range = jnp.max(-bc_all[c - 1:c, :])

    @pl.when(chunk_range <= GLA_RANGE_SAFE)
    def _():
        prep = []
        for b, pr in units:
            ks = slice(pr * LANES, (pr + 1) * LANES)
            q = pad(q_ref[b, :, ks])
            k = pad(k_ref[b, :, ks])
            bcum = bc_all[:, b * WGK + pr * LANES:b * WGK + (pr + 1) * LANES]
            bm = bcum[mid:mid + 1, :]
            bl = bcum[c - 1:c, :]
            prep.append(dict(
                qt=heads_on_rows(q * jnp.exp(bcum - bm)),
                qi=heads_on_rows(q * jnp.exp(bcum)),
                kt=(k * jnp.exp(bm - bcum)).astype(BF16),
                kd=(k * jnp.exp(bl - bcum)).astype(BF16),
                decay=jnp.exp(bl),
                v2=pad(v_ref[b, :, 2 * pr * DVG:2 * (pr + 1) * DVG]).astype(BF16)))

        amat = [jnp.where(causal, _dot_nt(u["qt"], u["kt"]), 0.0).astype(BF16) for u in prep]
        inter = [_dot_nt(u["qi"], st_scr[b, pr].astype(BF16)) for u, (b, pr) in zip(prep, units)]
        for u, a, it, (b, pr) in zip(prep, amat, inter, units):
            for hh in range(2):
                o = (_dot(a[hh * c:(hh + 1) * c, :], u["v2"][:, hh * DVG:(hh + 1) * DVG])
                     + it[hh * c:(hh + 1) * c, :])
                o_ref[b, :, (2 * pr + hh) * DVG:(2 * pr + hh + 1) * DVG] = o[0:c_in, :]
        for u, (b, pr) in zip(prep, units):
            upd = _dot_tn(u["v2"], u["kd"])
            st_scr[b, pr] = st_scr[b, pr] * u["decay"] + jnp.where(head0, upd[0:DVG, :], upd[DVG:2 * DVG, :])

    @pl.when(jnp.logical_not(chunk_range <= GLA_RANGE_SAFE))
    def _():
        row_head0 = lax.broadcasted_iota(jnp.int32, (LANES, 1), 0) < DKG

        def columns(x):
            return jnp.concatenate([x, jnp.zeros((LANES - c_in, LANES), F32)], axis=0).T

        def unit_body(u, carry):
            b = u // (HG // 2)
            pr = u % (HG // 2)
            ks = pl.ds(pl.multiple_of(pr * LANES, LANES), LANES)
            qc = columns(q_ref[b, :, ks])
            kc = columns(k_ref[b, :, ks])
            ac = columns(jnp.exp(g_ref[b, :, ks]))
            v2 = v_ref[b, :, pl.ds(pl.multiple_of(pr * 2 * DVG, 2 * DVG), 2 * DVG)]
            s = st_scr[b, pr].T
            for t in range(c_in):
                vrow = jnp.where(row_head0, v2[t:t + 1, 0:DVG], v2[t:t + 1, DVG:2 * DVG])
                s = s * ac[:, t:t + 1] + kc[:, t:t + 1] * vrow
                w = qc[:, t:t + 1] * s
                o_ref[b, pl.ds(t, 1), pl.ds(pl.multiple_of(pr * 2 * DVG, 2 * DVG), DVG)] = (
                    jnp.sum(w[0:DKG, :], axis=0, keepdims=True))
                o_ref[b, pl.ds(t, 1), pl.ds(pl.multiple_of(pr * 2 * DVG + DVG, DVG), DVG)] = (
                    jnp.sum(w[DKG:2 * DKG, :], axis=0, keepdims=True))
            st_scr[b, pr] = s.T
            return carry

        lax.fori_loop(0, len(units), unit_body, 0)

    @pl.when(ci == pl.num_programs(1) - 1)
    def _():
        for b, pr in units:
            sout_ref[b, pr] = st_scr[b, pr].T


def _gla(qg, kg, vg, la, s0, s0_layer, s_prev, layer, depth, batch, seq, nb, c_in, c):
    n_chunks = seq // c_in
    tril = jnp.tril(jnp.ones((c, c), BF16))
    tok = lambda w: pl.BlockSpec((nb, c_in, w), lambda bi, ci: (bi, ci, 0))
    st_out = pl.BlockSpec((None, nb, HG // 2, 2 * DKG, DVG), lambda bi, ci: (layer, bi, 0, 0, 0))
    st_in = pl.BlockSpec((None, nb, HG // 2, 2 * DKG, DVG), lambda bi, ci: (s0_layer, bi, 0, 0, 0))
    in_specs = [tok(WGK), tok(WGK), tok(WGV), tok(WGK), st_in, pl.BlockSpec((c, c), lambda bi, ci: (0, 0))]
    args = [qg.reshape(batch, seq, WGK), kg.reshape(batch, seq, WGK), vg.reshape(batch, seq, WGV),
            la.reshape(batch, seq, WGK), s0, tril]
    aliases = {}
    if s_prev is not None:
        in_specs.append(pl.BlockSpec(memory_space=pl.ANY))
        args.append(s_prev)
        aliases = {6: 1}
    o, s_all = pl.pallas_call(
        functools.partial(_gla_kernel, nb=nb, c_in=c_in, c=c),
        grid=(batch // nb, n_chunks),
        in_specs=in_specs,
        out_specs=[tok(WGV), st_out],
        out_shape=[jax.ShapeDtypeStruct((batch, seq, WGV), F32),
                   jax.ShapeDtypeStruct((depth, batch, HG // 2, 2 * DKG, DVG), F32)],
        scratch_shapes=[pltpu.VMEM((nb, HG // 2, 2 * DKG, DVG), F32)],
        input_output_aliases=aliases,
        compiler_params=_cparams(("parallel", "arbitrary")),
        name="gla",
    )(*args)
    return o.reshape(batch * seq, WGV), s_all


def _merge_kernel(x_ref, oa_ref, og_ref, rg_ref, gmix_ref, subln_ref, glan_ref,
                  wgate_ref, wdo_ref, wgo_ref, wout_ref, o_ref, ya_ref, yg_ref, *, lam_init):
    x = x_ref[...]
    h = _rms(x, gmix_ref[...]).astype(BF16)
    gate = jax.nn.sigmoid(_dot(h, wgate_ref[...]))
    for hd in range(HA):
        cs = slice(hd * LANES, (hd + 1) * LANES)
        ya_ref[:, cs] = (_rms(oa_ref[:, cs], subln_ref[...]) * (1.0 - lam_init)).astype(BF16)
        yg_ref[:, cs] = (_rms(og_ref[:, cs], glan_ref[...]) * jax.nn.silu(rg_ref[:, cs])).astype(BF16)
    ya = _dot(ya_ref[...], wdo_ref[...])
    yg = _dot(yg_ref[...], wgo_ref[...])
    mix = gate[:, 0:D_MODEL] * ya + gate[:, D_MODEL:2 * D_MODEL] * yg
    o_ref[...] = x + _dot(mix.astype(BF16), wout_ref[...])


def _merge(x, oa, og, rg, l, p, lam_init):
    t = x.shape[0]
    row = lambda w: pl.BlockSpec((TM, w), lambda i: (i, 0))
    lay2 = lambda w: pl.BlockSpec((None, 1, w), lambda i: (l, 0, 0))
    lay3 = lambda a, b: pl.BlockSpec((None, a, b), lambda i: (l, 0, 0), pipeline_mode=pl.Buffered(1))
    return pl.pallas_call(
        functools.partial(_merge_kernel, lam_init=lam_init),
        grid=(t // TM,),
        in_specs=[row(D_MODEL), row(WA), row(WGV), row(WGV), lay2(D_MODEL), lay2(2 * DHA), lay2(DVG),
                  lay3(D_MODEL, 2 * D_MODEL), lay3(WA, D_MODEL), lay3(WGV, D_MODEL), lay3(D_MODEL, D_MODEL)],
        out_specs=row(D_MODEL),
        out_shape=jax.ShapeDtypeStruct((t, D_MODEL), F32),
        scratch_shapes=[pltpu.VMEM((TM, WA), BF16), pltpu.VMEM((TM, WGV), BF16)],
        compiler_params=_cparams(("parallel",)),
        name="merge",
    )(x, oa, og, rg, p["norm_mix"], p["subln"], p["gla_norm"],
      p["w_gate"], p["w_diff_out"], p["w_gla_out"], p["w_out"])


def _ffn_kernel(x_ref, g_ref, wg_ref, wu_ref, wd_ref, o_ref, *, chunk):
    x = x_ref[...]
    h = _rms(x, g_ref[...]).astype(BF16)
    acc = x
    for c0 in range(0, wg_ref.shape[1], chunk):
        a = jax.nn.silu(_dot(h, wg_ref[:, c0:c0 + chunk])) * _dot(h, wu_ref[:, c0:c0 + chunk])
        acc = acc + _dot(a.astype(BF16), wd_ref[c0:c0 + chunk, :])
    o_ref[...] = acc


def _ffn(x, l, p):
    t = x.shape[0]
    f = p["w_ffn_gate"].shape[2]
    row = pl.BlockSpec((TM, D_MODEL), lambda i: (i, 0))
    once = pl.Buffered(1)
    return pl.pallas_call(
        functools.partial(_ffn_kernel, chunk=f // 2),
        grid=(t // TM,),
        in_specs=[row, pl.BlockSpec((None, 1, D_MODEL), lambda i: (l, 0, 0)),
                  pl.BlockSpec((None, D_MODEL, f), lambda i: (l, 0, 0), pipeline_mode=once),
                  pl.BlockSpec((None, D_MODEL, f), lambda i: (l, 0, 0), pipeline_mode=once),
                  pl.BlockSpec((None, f, D_MODEL), lambda i: (l, 0, 0), pipeline_mode=once)],
        out_specs=row,
        out_shape=jax.ShapeDtypeStruct((t, D_MODEL), F32),
        compiler_params=_cparams(("parallel",)),
        name="ffn",
    )(x, p["norm_ffn"], p["w_ffn_gate"], p["w_ffn_up"], p["w_ffn_down"])


def kernel(x_prompt, x_sample, cache_k, cache_v, state_gla, page_table, rel_bias, norm_mix, w_in, w_alpha2, b_alpha, qn_gain, kn_gain, lam_q1, lam_k1, lam_q2, lam_k2, subln_gain, gla_norm_gain, w_diff_out, w_gla_out, w_out, norm_ffn, w_ffn_gate, w_ffn_up, w_ffn_down):
    batch, seq, _ = x_prompt.shape
    dec_batch, ts, _ = x_sample.shape
    n_pages = page_table.shape[1]
    n_pool = cache_k.shape[1]
    past = n_pages * PAGE_SIZE
    depth = w_in.shape[0]
    gate0 = MAIN_W + GATE_RANK

    vec = lambda a: a.reshape(depth, 1, a.shape[-1])
    p = {
        "norm_mix": vec(norm_mix), "norm_ffn": vec(norm_ffn), "b_alpha": vec(b_alpha),
        "gq": vec(jnp.tile(qn_gain, (1, WA // DHA))), "gk": vec(jnp.tile(kn_gain, (1, WA // DHA))),
        "subln": vec(subln_gain), "gla_norm": vec(gla_norm_gain),
        "w_main": w_in[:, :, :MAIN_W].astype(BF16),
        "w_alr": jnp.pad(w_in[:, :, MAIN_W:gate0], ((0, 0), (0, 0), (0, LANES - GATE_RANK))).astype(BF16),
        "w_gate": w_in[:, :, gate0:].astype(BF16),
        "w_alpha2": jnp.pad(w_alpha2, ((0, 0), (0, LANES - GATE_RANK), (0, 0))).astype(BF16),
        "w_diff_out": w_diff_out.astype(BF16), "w_gla_out": w_gla_out.astype(BF16), "w_out": w_out.astype(BF16),
        "w_ffn_gate": w_ffn_gate.astype(BF16), "w_ffn_up": w_ffn_up.astype(BF16),
        "w_ffn_down": w_ffn_down.astype(BF16),
        "gsum": jnp.kron(jnp.eye(WA // DHA, dtype=F32), jnp.full((DHA, DHA), 1.0 / DHA, F32)).astype(BF16),
    }
    lamp = jnp.stack([lam_q1, lam_k1, lam_q2, lam_k2], axis=1)

    ckt = jnp.transpose(cache_k, (0, 1, 3, 4, 5, 2)).reshape(depth, n_pool, WA, PAGE_SIZE)
    cv4 = cache_v.reshape(depth, n_pool, PAGE_SIZE * HA, LANES)
    sg = state_gla.reshape(depth, dec_batch, HG // 2, 2 * DKG, DVG)
    s0p = jnp.zeros((1, batch, HG // 2, 2 * DKG, DVG), F32)

    bias_p = _bias_tables(rel_bias, 3, TK, TQ, TQ, 0, TK, True, True)
    bias_s = _bias_tables(rel_bias, 1, 2 * ts, past + PAGE_SIZE, ts, past, 0, False, False)
    bias_s = bias_s.reshape(HA * 2 * ts, past + PAGE_SIZE)
    flags = _logit_bound_flags(qn_gain, kn_gain, rel_bias)

    xp = x_prompt.reshape(batch * seq, D_MODEL)
    xs = x_sample.reshape(dec_batch * ts, D_MODEL)
    kv = ()
    kv_s = ()
    sp_all = ss_all = None
    for l in range(depth):
        lam_init = 0.8 - 0.6 * math.exp(-0.3 * l)
        qa, kt_all, v4_all, qg, kg, vg, rg, la = _inproj(xp, l, p, True, kv, batch, seq)
        kv = (kt_all, v4_all)
        oa = _attn_prompt(qa, kt_all, v4_all, bias_p, flags, lamp[l], l, lam_init, batch, seq)
        og, sp_all = _gla(qg, kg, vg, la, s0p, 0, sp_all, l, depth, batch, seq, batch, GLA_C, GLA_C)
        xp = _merge(xp, oa, og, rg, l, p, lam_init)
        xp = _ffn(xp, l, p)
        qa, ka, va, kst_all, vs4_all, qg, kg, vg, rg, la = _inproj(xs, l, p, False, kv_s, dec_batch, ts)
        kv_s = (kst_all, vs4_all)
        oa = _attn_sample(qa, ka, va, ckt, cv4, page_table, bias_s, lamp[l], l, lam_init, dec_batch, ts)
        og, ss_all = _gla(qg, kg, vg, la, sg, l, ss_all, l, depth, dec_batch, ts, GLA_SB, ts, 2 * ts)
        xs = _merge(xs, oa, og, rg, l, p, lam_init)
        xs = _ffn(xs, l, p)
    kt_all, v4_all = kv
    kst_all, vs4_all = kv_s
    k_prompt = jnp.transpose(kt_all.reshape(depth, batch, HA, 2, DHA, seq), (0, 1, 5, 2, 3, 4))
    v_prompt = v4_all.reshape(depth, batch, seq, HA, 2 * DHA)
    k_sample = jnp.transpose(kst_all.reshape(depth, ts, HA, 2, DHA, dec_batch), (0, 5, 1, 2, 3, 4))
    v_sample = vs4_all.reshape(depth, dec_batch, ts, HA, 2 * DHA)
    return (xp.reshape(batch, seq, D_MODEL), xs.reshape(dec_batch, ts, D_MODEL),
            k_prompt, v_prompt, sp_all.reshape(depth, batch, HG, DKG, DVG),
            k_sample, v_sample, ss_all.reshape(depth, dec_batch, HG, DKG, DVG))
```

```python
import functools
import math

import jax
import jax.numpy as jnp
from jax import lax
from jax.experimental import pallas as pl
from jax.experimental.pallas import tpu as pltpu

F32 = jnp.float32
BF16 = jnp.bfloat16

D_MODEL = 1024
HA = 4
DHA = 64
HG = 4
DKG = 64
DVG = 128
GATE_RANK = 16
GATE_NORM = 16.0
N_BUCKETS = 32
MAX_DISTANCE = 128
PAGE_SIZE = 128
EPS = 1e-6
WA = HA * 2 * DHA
WGK = HG * DKG
WGV = HG * DVG
MAIN_W = 3 * WA + 2 * WGK + 2 * WGV
LANES = 128
NEG = -1e30
LOG2E = math.log2(math.e)
GLA_RANGE_SAFE = 80.0
LOGIT_SAFE = 60.0

TM = 512
TQ = 512
TK = 512
GLA_C = 64
GLA_SB = 8
VMEM_LIMIT = 56 * 1024 * 1024


def _cparams(sem):
    return pltpu.CompilerParams(dimension_semantics=sem, vmem_limit_bytes=VMEM_LIMIT)


def _rms(x, gain):
    ms = jnp.mean(x * x, axis=-1, keepdims=True)
    return x * lax.rsqrt(ms + EPS) * gain


def _dot(a, b):
    return jnp.dot(a, b, preferred_element_type=F32)


def _dot_nt(a, b):
    return lax.dot_general(a, b, (((1,), (1,)), ((), ())), preferred_element_type=F32)


def _dot_tn(a, b):
    return lax.dot_general(a, b, (((0,), (0,)), ((), ())), preferred_element_type=F32)


def _lam(lamp_ref, lam_init):
    a = jnp.sum(lamp_ref[0:1, :] * lamp_ref[1:2, :], axis=-1, keepdims=True)
    b = jnp.sum(lamp_ref[2:3, :] * lamp_ref[3:4, :], axis=-1, keepdims=True)
    return jnp.exp(a) - jnp.exp(b) + lam_init


def _bias_kernel(rb_ref, o_ref, *, period, off0, off_step, transposed, shifted):
    h = pl.program_id(0)
    t = pl.program_id(1)
    rows, cols = o_ref.shape
    r = lax.broadcasted_iota(jnp.int32, (rows, cols), 0)
    c = lax.broadcasted_iota(jnp.int32, (rows, cols), 1)
    if transposed:
        r, c = c, r
    d = off0 + t * off_step + jnp.bitwise_and(r, period - 1) - c
    n = jnp.maximum(d, 0)
    max_exact = N_BUCKETS // 2
    nf = jnp.maximum(n, 1).astype(F32)
    large = max_exact + (jnp.log(nf / max_exact) / math.log(MAX_DISTANCE / max_exact)
                         * (N_BUCKETS - max_exact)).astype(jnp.int32)
    large = jnp.minimum(large, N_BUCKETS - 1)
    bucket = jnp.where(n < max_exact, n, large)
    val = jnp.zeros((rows, cols), F32)
    for k in range(N_BUCKETS):
        val = jnp.where(bucket == k, rb_ref[k, h], val)
    if shifted:
        val = val - rb_ref[N_BUCKETS - 1, h]
    o_ref[...] = jnp.where(d >= 0, val * LOG2E, NEG)


def _bias_tables(rel_bias, n_t, rows, cols, period, off0, off_step, transposed, shifted):
    return pl.pallas_call(
        functools.partial(_bias_kernel, period=period, off0=off0, off_step=off_step,
                          transposed=transposed, shifted=shifted),
        grid=(HA, n_t),
        in_specs=[pl.BlockSpec(memory_space=pltpu.SMEM)],
        out_specs=pl.BlockSpec((None, None, rows, cols), lambda h, t: (h, t, 0, 0)),
        out_shape=jax.ShapeDtypeStruct((HA, n_t, rows, cols), F32),
        compiler_params=_cparams(("arbitrary", "arbitrary")),
        name="bias_tables",
    )(rel_bias)


def _inproj_kernel(*refs, prompt, n_alias, ts):
    (x_ref, gmix_ref, w_ref, walr_ref, wa2_ref, ba_ref, gq_ref, gk_ref, gsum_ref) = refs[:9]
    outs = refs[9 + n_alias:]
    if prompt:
        (qa_ref, ka_ref, va_ref, kb_ref, vtb_ref, qg_ref, kg_ref, vg_ref, rg_ref, la_ref) = outs
    else:
        (qa_ref, ka_ref, va_ref, kst_ref, vs4_ref, qg_ref, kg_ref, vg_ref, rg_ref, la_ref, kscr_ref) = outs
    h = _rms(x_ref[...], gmix_ref[...]).astype(BF16)
    proj = _dot_nt(h, w_ref[...])
    gs = gsum_ref[...]
    qa = proj[:, 0:WA]
    ka = proj[:, WA:2 * WA]
    qms = _dot((qa * qa).astype(BF16), gs)
    kms = _dot((ka * ka).astype(BF16), gs)
    qa_ref[...] = (qa * lax.rsqrt(qms + EPS) * gq_ref[...] * (DHA ** -0.5 * LOG2E)).astype(BF16)
    kan = ka * lax.rsqrt(kms + EPS) * gk_ref[...]
    if prompt:
        ka_ref[...] = kan.T
        kb_ref[...] = kan.astype(BF16)
        vtb_ref[...] = proj[:, 2 * WA:3 * WA].T.astype(BF16)
        for hd in range(HA):
            va_ref[pl.ds(hd, TM, stride=HA), :] = proj[:, 2 * WA + hd * LANES:2 * WA + (hd + 1) * LANES]
    else:
        ka_ref[...] = kan
        va_ref[...] = proj[:, 2 * WA:3 * WA]
        for hd in range(HA):
            vs4_ref[pl.ds(hd, TM, stride=HA), :] = proj[:, 2 * WA + hd * LANES:2 * WA + (hd + 1) * LANES]
        for j in range(WA // LANES):
            kscr_ref[j] = kan[:, j * LANES:(j + 1) * LANES]
        nbt = TM // ts
        for tile in range(kst_ref.shape[2] // nbt):
            @pl.when(pl.program_id(0) == tile)
            def _(tile=tile):
                lo, hi = tile * nbt, (tile + 1) * nbt
                for t in range(ts):
                    for j in range(WA // LANES):
                        rows = kscr_ref[j, pl.ds(t, nbt, stride=ts), :]
                        parts = [jnp.zeros((n, LANES), F32) for n in (lo,) if n] + [rows]
                        parts += [jnp.zeros((n, LANES), F32) for n in (LANES - hi,) if n]
                        tr = jnp.concatenate(parts, axis=0).T
                        kst_ref[t, j * LANES:(j + 1) * LANES, lo:hi] = tr[:, lo:hi]
    o = 3 * WA
    qg_ref[...] = proj[:, o:o + WGK] * (DKG ** -0.5)
    kg_ref[...] = proj[:, o + WGK:o + 2 * WGK]
    o += 2 * WGK
    vg_ref[...] = proj[:, o:o + WGV]
    rg_ref[...] = proj[:, o + WGV:o + 2 * WGV]
    alr = _dot_nt(h, walr_ref[...])
    z = _dot(alr.astype(BF16), wa2_ref[...]) + ba_ref[...]
    log_sig = jnp.minimum(z, 0.0) - jnp.log(1.0 + jnp.exp(-jnp.abs(z)))
    la_ref[...] = log_sig * (1.0 / GATE_NORM)


def _inproj(x, l, p, prompt, kv_prev, batch, seq):
    t = x.shape[0]
    depth = p["w_in_t"].shape[0]
    row = lambda w: pl.BlockSpec((TM, w), lambda i: (i, 0))
    lay2 = lambda w: pl.BlockSpec((None, 1, w), lambda i: (l, 0, 0))
    lay3 = lambda a, b: pl.BlockSpec((None, a, b), lambda i: (l, 0, 0), pipeline_mode=pl.Buffered(1))
    in_specs = [row(D_MODEL), lay2(D_MODEL), lay3(MAIN_W, D_MODEL),
                pl.BlockSpec((None, GATE_RANK, D_MODEL), lambda i: (l, MAIN_W // GATE_RANK, 0)),
                lay3(GATE_RANK, WGK), lay2(WGK), lay2(WA), lay2(WA),
                pl.BlockSpec((WA, WA), lambda i: (0, 0))]
    args = [x, p["norm_mix"], p["w_in_t"], p["w_in_t"], p["w_alpha2"], p["b_alpha"], p["gq"], p["gk"], p["gsum"]]
    rest = [(WGK, F32), (WGK, F32), (WGV, F32), (WGV, F32), (WGK, F32)]
    if prompt:
        spb = seq // TM
        kv_specs = [pl.BlockSpec((None, None, WA, TM), lambda i: (l, i // spb, 0, i % spb)),
                    pl.BlockSpec((None, TM * HA, LANES), lambda i: (l, i, 0)),
                    row(WA),
                    pl.BlockSpec((None, WA, TM), lambda i: (i // spb, 0, i % spb))]
        kv_shapes = [jax.ShapeDtypeStruct((depth, batch, WA, seq), F32),
                     jax.ShapeDtypeStruct((depth, t * HA, LANES), F32),
                     jax.ShapeDtypeStruct((t, WA), BF16),
                     jax.ShapeDtypeStruct((batch, WA, seq), BF16)]
        first_kv = 1
        scratch = []
    else:
        kv_specs = [row(WA), row(WA),
                    pl.BlockSpec((None, seq, WA, batch), lambda i: (l, 0, 0, 0)),
                    pl.BlockSpec((None, TM * HA, LANES), lambda i: (l, i, 0))]
        kv_shapes = [jax.ShapeDtypeStruct((t, WA), F32)] * 2 + [
            jax.ShapeDtypeStruct((depth, seq, WA, batch), F32),
            jax.ShapeDtypeStruct((depth, t * HA, LANES), F32)]
        first_kv = 3
        scratch = [pltpu.VMEM((WA // LANES, TM, LANES), F32)]
    aliases = {}
    for n, a in enumerate(kv_prev):
        in_specs.append(pl.BlockSpec(memory_space=pl.ANY))
        args.append(a)
        aliases[9 + n] = first_kv + n
    return pl.pallas_call(
        functools.partial(_inproj_kernel, prompt=prompt, n_alias=len(aliases), ts=seq),
        grid=(t // TM,),
        in_specs=in_specs,
        out_specs=[row(WA)] + kv_specs + [row(w) for w, _ in rest],
        out_shape=[jax.ShapeDtypeStruct((t, WA), BF16)] + kv_shapes
                  + [jax.ShapeDtypeStruct((t, w), dt) for w, dt in rest],
        scratch_shapes=scratch,
        input_output_aliases=aliases,
        compiler_params=_cparams(("parallel",) if prompt else ("arbitrary",)),
        name="inproj",
    )(*args)


def _bound_kernel(gq_ref, gk_ref, rb_ref, o_ref):
    mq = jnp.max(jnp.abs(gq_ref[...]), axis=1, keepdims=True)
    mk = jnp.max(jnp.abs(gk_ref[...]), axis=1, keepdims=True)
    rb = rb_ref[...]
    shifted = jnp.abs(rb - rb[N_BUCKETS - 1:N_BUCKETS, :])
    bmax = jnp.max(jnp.max(shifted, axis=1, keepdims=True), axis=0, keepdims=True)
    bound = LOG2E * (1.01 * DHA ** 0.5 * mq * mk + bmax)
    o_ref[...] = jnp.broadcast_to((bound <= LOGIT_SAFE).astype(jnp.int32), o_ref.shape)


def _logit_bound_flags(qn_gain, kn_gain, rel_bias):
    depth = qn_gain.shape[0]
    return pl.pallas_call(
        _bound_kernel,
        out_shape=jax.ShapeDtypeStruct((depth, LANES), jnp.int32),
        name="logit_bound",
    )(qn_gain, kn_gain, rel_bias)


def _attn_kernel(flag_ref, lamp_ref, q_ref, kb_ref, vtb_ref, bias_ref, o_ref,
                 qs_ref, m_ref, l_ref, acc_ref, sa_ref, sb_ref, *, lam_init, seq, layer):
    nq = seq // TQ
    bounded = flag_ref[layer, 0] == 1

    lane = lax.broadcasted_iota(jnp.int32, (1, LANES), 1)
    lam = _lam(lamp_ref, lam_init)

    def logits(j, dst_ref):
        start = pl.multiple_of(j * TK, TK)
        dst_ref[...] = _dot_nt(kb_ref[pl.ds(start, TK), :], qs_ref[...])

    def start_block(i):
        q = q_ref[pl.ds(pl.multiple_of(i * TQ, TQ), TQ), :]
        qs_ref[0:TQ, :] = jnp.where(lane < DHA, q, jnp.zeros_like(q))
        qs_ref[TQ:2 * TQ, :] = jnp.where(lane >= DHA, q, jnp.zeros_like(q))
        m_ref[...] = jnp.full(m_ref.shape, NEG, F32)
        l_ref[...] = jnp.zeros(l_ref.shape, F32)
        acc_ref[...] = jnp.zeros(acc_ref.shape, F32)
        logits(jnp.where(bounded, i, 0), sa_ref)

    start_block(0)


    def biased(src_ref, table):
        b = bias_ref[table]
        return jnp.concatenate([src_ref[:, 0:TQ] + b, src_ref[:, TQ:2 * TQ] + b], axis=1)

    def accumulate(p, j):
        start = pl.multiple_of(j * TK, TK)
        l_ref[...] += jnp.sum(p, axis=0, keepdims=True)
        acc_ref[...] += _dot(vtb_ref[:, pl.ds(start, TK)], p.astype(BF16))

    def q_block(i, carry):
        jl = i

        @pl.when(bounded)
        def _():
            @pl.when(jl == 0)
            def _():
                accumulate(jnp.exp2(biased(sa_ref, 0)), jl)

            @pl.when(jl >= 1)
            def _():
                logits(jl - 1, sb_ref)
                accumulate(jnp.exp2(biased(sa_ref, 0)), jl)
                logits(0, sa_ref)
                accumulate(jnp.exp2(biased(sb_ref, 1)), jl - 1)

            n_far = jnp.maximum(jl - 1, 0)

            def far_pair(jj, carry):
                j0 = 2 * jj
                logits(jnp.minimum(j0 + 1, jl), sb_ref)
                accumulate(jnp.exp2(sa_ref[...]), j0)

                @pl.when(j0 + 1 < n_far)
                def _():
                    logits(jnp.minimum(j0 + 2, jl), sa_ref)
                    accumulate(jnp.exp2(sb_ref[...]), j0 + 1)

                return carry

            lax.fori_loop(0, (n_far + 1) // 2, far_pair, 0)

        @pl.when(jnp.logical_not(bounded))
        def _():
            def update(src_ref, j):
                table = jnp.minimum(jl - j, 2)
                start = pl.multiple_of(j * TK, TK)
                s = biased(src_ref, table)
                m_prev = m_ref[...]
                m_new = jnp.maximum(m_prev, jnp.max(s, axis=0, keepdims=True))
                alpha = jnp.exp2(m_prev - m_new)
                p = jnp.exp2(s - m_new)
                l_ref[...] = alpha * l_ref[...] + jnp.sum(p, axis=0, keepdims=True)
                acc_ref[...] = alpha * acc_ref[...] + _dot(vtb_ref[:, pl.ds(start, TK)], p.astype(BF16))
                m_ref[...] = m_new

            def pair_body(jj, carry):
                j0 = 2 * jj
                logits(jnp.minimum(j0 + 1, jl), sb_ref)
                update(sa_ref, j0)

                @pl.when(j0 + 1 <= jl)
                def _():
                    logits(jnp.minimum(j0 + 2, jl), sa_ref)
                    update(sb_ref, j0 + 1)

                return carry

            lax.fori_loop(0, jl // 2 + 1, pair_body, 0)

        o = acc_ref[...] * (1.0 / l_ref[...])
        o_ref[pl.ds(pl.multiple_of(i * TQ, TQ), TQ), :] = (o[:, 0:TQ] - lam * o[:, TQ:2 * TQ]).T
        start_block(jnp.minimum(i + 1, nq - 1))
        return carry

    lax.fori_loop(0, nq, q_block, 0)


def _attn_prompt(qa, kb, vtb, bias, flags, lamp, l, lam_init, batch, seq):
    q3 = qa.reshape(batch, seq, WA)
    k3 = kb.reshape(batch, seq, WA)
    blk = pl.BlockSpec((None, seq, LANES), lambda b, h: (b, 0, h))
    out = pl.pallas_call(
        functools.partial(_attn_kernel, lam_init=lam_init, seq=seq, layer=l),
        grid=(batch, HA),
        in_specs=[pl.BlockSpec(memory_space=pltpu.SMEM),
                  pl.BlockSpec((4, DHA), lambda b, h: (0, 0)),
                  blk, blk,
                  pl.BlockSpec((None, LANES, seq), lambda b, h: (b, h, 0)),
                  pl.BlockSpec((None, 3, TK, TQ), lambda b, h: (h, 0, 0, 0))],
        out_specs=blk,
        out_shape=jax.ShapeDtypeStruct((batch, seq, WA), F32),
        scratch_shapes=[pltpu.VMEM((2 * TQ, LANES), BF16),
                        pltpu.VMEM((1, 2 * TQ), F32), pltpu.VMEM((1, 2 * TQ), F32),
                        pltpu.VMEM((LANES, 2 * TQ), F32),
                        pltpu.VMEM((TK, 2 * TQ), F32), pltpu.VMEM((TK, 2 * TQ), F32)],
        compiler_params=_cparams(("parallel", "parallel")),
        name="attn_prompt",
    )(flags, lamp, q3, k3, vtb, bias)
    return out.reshape(batch * seq, WA)


def _attn_sample_kernel(pt_ref, lamp_ref, q_ref, kn_ref, vn_ref, bias_ref, *rest, n_pages, ts, lam_init):
    kp = rest[:n_pages]
    vp = rest[n_pages:2 * n_pages]
    o_ref = rest[2 * n_pages]
    s_ref = rest[2 * n_pages + 1]
    nr = HA * 2 * ts
    past = n_pages * PAGE_SIZE

    q = q_ref[...].astype(F32)
    qt = jnp.concatenate([q] * (2 * HA), axis=0)
    r = lax.broadcasted_iota(jnp.int32, (nr, WA), 0)
    c = lax.broadcasted_iota(jnp.int32, (nr, WA), 1)
    keep = lax.shift_right_logical(c, int(math.log2(DHA))) == lax.shift_right_logical(r, int(math.log2(ts)))
    qbd = jnp.where(keep, qt, 0.0).astype(BF16)
    zpad = jnp.zeros((PAGE_SIZE - ts, WA), F32)
    knp = jnp.concatenate([kn_ref[...], zpad], axis=0).astype(BF16)
    vnp = jnp.concatenate([vn_ref[...], zpad], axis=0).astype(BF16)

    for pg in range(n_pages):
        s_ref[:, pg * PAGE_SIZE:(pg + 1) * PAGE_SIZE] = _dot(qbd, kp[pg][...].astype(BF16))
    s_ref[:, past:past + PAGE_SIZE] = _dot_nt(qbd, knp)

    s = s_ref[...] + bias_ref[...]
    m = jnp.max(s, axis=-1, keepdims=True)
    p = jnp.exp2(s - m)
    l = jnp.sum(p, axis=-1, keepdims=True)
    pb = p.astype(BF16)
    lam = _lam(lamp_ref, lam_init)
    for h in range(HA):
        rs = slice(h * 2 * ts, (h + 1) * 2 * ts)
        cs = slice(h * LANES, (h + 1) * LANES)
        acc = _dot(pb[rs, past:past + PAGE_SIZE], vnp[:, cs])
        for pg in range(n_pages):
            vh = vp[pg][pl.ds(h, PAGE_SIZE, stride=HA), :].astype(BF16)
            acc = acc + _dot(pb[rs, pg * PAGE_SIZE:(pg + 1) * PAGE_SIZE], vh)
        acc = acc / l[rs]
        o_ref[:, cs] = acc[0:ts, :] - lam * acc[ts:2 * ts, :]


def _attn_sample(qa, ka, va, ckt, cv4, page_table, bias, lamp, l, lam_init, dec_batch, ts):
    n_pages = page_table.shape[1]
    nr = HA * 2 * ts
    width = n_pages * PAGE_SIZE + PAGE_SIZE
    tok = pl.BlockSpec((None, ts, WA), lambda b, pt: (b, 0, 0))
    page = lambda pg: pl.BlockSpec((None, None, HA * PAGE_SIZE, LANES), lambda b, pt: (l, pt[b, pg], 0, 0))
    grid_spec = pltpu.PrefetchScalarGridSpec(
        num_scalar_prefetch=1,
        grid=(dec_batch,),
        in_specs=[pl.BlockSpec((4, DHA), lambda b, pt: (0, 0)), tok, tok, tok,
                  pl.BlockSpec((nr, width), lambda b, pt: (0, 0))]
                 + [page(pg) for pg in range(n_pages)] * 2,
        out_specs=tok,
        scratch_shapes=[pltpu.VMEM((nr, width), F32)],
    )
    out = pl.pallas_call(
        functools.partial(_attn_sample_kernel, n_pages=n_pages, ts=ts, lam_init=lam_init),
        grid_spec=grid_spec,
        out_shape=jax.ShapeDtypeStruct((dec_batch, ts, WA), F32),
        compiler_params=_cparams(("parallel",)),
        name="attn_sample",
    )(page_table, lamp, qa.reshape(dec_batch, ts, WA), ka.reshape(dec_batch, ts, WA),
      va.reshape(dec_batch, ts, WA), bias, *([ckt] * n_pages), *([cv4] * n_pages))
    return out.reshape(dec_batch * ts, WA)


def _split3(x):
    x1 = x.astype(BF16)
    r1 = x - x1.astype(F32)
    x2 = r1.astype(BF16)
    x3 = (r1 - x2.astype(F32)).astype(BF16)
    return x1, x2, x3


def _gla_kernel(q_ref, k_ref, v_ref, g_ref, s0_ref, tril_ref, *rest, nb, c_in, c):
    o_ref, sout_ref, st_scr = rest[-3:]
    ci = pl.program_id(1)
    units = [(b, pr) for b in range(nb) for pr in range(HG // 2)]

    @pl.when(ci == 0)
    def _():
        for b, pr in units:
            st_scr[b, pr] = s0_ref[b, pr].T

    def pad(x):
        if c_in == c:
            return x
        return jnp.concatenate([x, jnp.zeros((c - c_in, x.shape[1]), x.dtype)], axis=0)

    lane = lax.broadcasted_iota(jnp.int32, (1, LANES), 1)
    head0 = lane < DKG
    tt = lax.broadcasted_iota(jnp.int32, (2 * c, c), 0)
    ss = lax.broadcasted_iota(jnp.int32, (2 * c, c), 1)
    causal = jnp.bitwise_and(tt, c - 1) >= ss
    mid = c // 2 - 1

    def heads_on_rows(x):
        return jnp.concatenate([jnp.where(head0, x, 0.0), jnp.where(head0, 0.0, x)], axis=0).astype(BF16)

    g_all = jnp.concatenate([pad(g_ref[b]) for b in range(nb)], axis=1)
    tril = tril_ref[...]
    g3 = _split3(g_all)
    bc_all = _dot(tril, g3[0]) + _dot(tril, g3[1]) + _dot(tril, g3[2])

    chunk_range = jnp.max(-bc_all[c - 1:c, :])

    @pl.when(chunk_range <= GLA_RANGE_SAFE)
    def _():
        prep = []
        for b, pr in units:
            ks = slice(pr * LANES, (pr + 1) * LANES)
            q = pad(q_ref[b, :, ks])
            k = pad(k_ref[b, :, ks])
            bcum = bc_all[:, b * WGK + pr * LANES:b * WGK + (pr + 1) * LANES]
            bm = bcum[mid:mid + 1, :]
            bl = bcum[c - 1:c, :]
            prep.append(dict(
                qt=heads_on_rows(q * jnp.exp(bcum - bm)),
                qi=heads_on_rows(q * jnp.exp(bcum)),
                kt=(k * jnp.exp(bm - bcum)).astype(BF16),
                kd=(k * jnp.exp(bl - bcum)).astype(BF16),
                decay=jnp.exp(bl),
                v2=pad(v_ref[b, :, 2 * pr * DVG:2 * (pr + 1) * DVG]).astype(BF16)))

        amat = [jnp.where(causal, _dot_nt(u["qt"], u["kt"]), 0.0).astype(BF16) for u in prep]
        inter = [_dot_nt(u["qi"], st_scr[b, pr].astype(BF16)) for u, (b, pr) in zip(prep, units)]
        for u, a, it, (b, pr) in zip(prep, amat, inter, units):
            for hh in range(2):
                o = (_dot(a[hh * c:(hh + 1) * c, :], u["v2"][:, hh * DVG:(hh + 1) * DVG])
                     + it[hh * c:(hh + 1) * c, :])
                o_ref[b, :, (2 * pr + hh) * DVG:(2 * pr + hh + 1) * DVG] = o[0:c_in, :]
        for u, (b, pr) in zip(prep, units):
            upd = _dot_tn(u["v2"], u["kd"])
            st_scr[b, pr] = st_scr[b, pr] * u["decay"] + jnp.where(head0, upd[0:DVG, :], upd[DVG:2 * DVG, :])

    @pl.when(jnp.logical_not(chunk_range <= GLA_RANGE_SAFE))
    def _():
        row_head0 = lax.broadcasted_iota(jnp.int32, (LANES, 1), 0) < DKG

        def columns(x):
            return jnp.concatenate([x, jnp.zeros((LANES - c_in, LANES), F32)], axis=0).T

        def unit_body(u, carry):
            b = u // (HG // 2)
            pr = u % (HG // 2)
            ks = pl.ds(pl.multiple_of(pr * LANES, LANES), LANES)
            qc = columns(q_ref[b, :, ks])
            kc = columns(k_ref[b, :, ks])
            ac = columns(jnp.exp(g_ref[b, :, ks]))
            v2 = v_ref[b, :, pl.ds(pl.multiple_of(pr * 2 * DVG, 2 * DVG), 2 * DVG)]
            s = st_scr[b, pr].T
            for t in range(c_in):
                vrow = jnp.where(row_head0, v2[t:t + 1, 0:DVG], v2[t:t + 1, DVG:2 * DVG])
                s = s * ac[:, t:t + 1] + kc[:, t:t + 1] * vrow
                w = qc[:, t:t + 1] * s
                o_ref[b, pl.ds(t, 1), pl.ds(pl.multiple_of(pr * 2 * DVG, 2 * DVG), DVG)] = (
                    jnp.sum(w[0:DKG, :], axis=0, keepdims=True))
                o_ref[b, pl.ds(t, 1), pl.ds(pl.multiple_of(pr * 2 * DVG + DVG, DVG), DVG)] = (
                    jnp.sum(w[DKG:2 * DKG, :], axis=0, keepdims=True))
            st_scr[b, pr] = s.T
            return carry

        lax.fori_loop(0, len(units), unit_body, 0)

    @pl.when(ci == pl.num_programs(1) - 1)
    def _():
        for b, pr in units:
            sout_ref[b, pr] = st_scr[b, pr].T


def _gla(qg, kg, vg, la, s0, s0_layer, s_prev, layer, depth, batch, seq, nb, c_in, c):
    n_chunks = seq // c_in
    tril = jnp.tril(jnp.ones((c, c), BF16))
    tok = lambda w: pl.BlockSpec((nb, c_in, w), lambda bi, ci: (bi, ci, 0))
    st_out = pl.BlockSpec((None, nb, HG // 2, 2 * DKG, DVG), lambda bi, ci: (layer, bi, 0, 0, 0))
    st_in = pl.BlockSpec((None, nb, HG // 2, 2 * DKG, DVG), lambda bi, ci: (s0_layer, bi, 0, 0, 0))
    in_specs = [tok(WGK), tok(WGK), tok(WGV), tok(WGK), st_in, pl.BlockSpec((c, c), lambda bi, ci: (0, 0))]
    args = [qg.reshape(batch, seq, WGK), kg.reshape(batch, seq, WGK), vg.reshape(batch, seq, WGV),
            la.reshape(batch, seq, WGK), s0, tril]
    aliases = {}
    if s_prev is not None:
        in_specs.append(pl.BlockSpec(memory_space=pl.ANY))
        args.append(s_prev)
        aliases = {6: 1}
    o, s_all = pl.pallas_call(
        functools.partial(_gla_kernel, nb=nb, c_in=c_in, c=c),
        grid=(batch // nb, n_chunks),
        in_specs=in_specs,
        out_specs=[tok(WGV), st_out],
        out_shape=[jax.ShapeDtypeStruct((batch, seq, WGV), F32),
                   jax.ShapeDtypeStruct((depth, batch, HG // 2, 2 * DKG, DVG), F32)],
        scratch_shapes=[pltpu.VMEM((nb, HG // 2, 2 * DKG, DVG), F32)],
        input_output_aliases=aliases,
        compiler_params=_cparams(("parallel", "arbitrary")),
        name="gla",
    )(*args)
    return o.reshape(batch * seq, WGV), s_all


def _merge_kernel(x_ref, oa_ref, og_ref, rg_ref, gmix_ref, subln_ref, glan_ref,
                  wgate_ref, wdo_ref, wgo_ref, wout_ref, o_ref, ya_ref, yg_ref, *, lam_init):
    x = x_ref[...]
    h = _rms(x, gmix_ref[...]).astype(BF16)
    gate = jax.nn.sigmoid(_dot_nt(h, wgate_ref[...]))
    for hd in range(HA):
        cs = slice(hd * LANES, (hd + 1) * LANES)
        ya_ref[:, cs] = (_rms(oa_ref[:, cs], subln_ref[...]) * (1.0 - lam_init)).astype(BF16)
        yg_ref[:, cs] = (_rms(og_ref[:, cs], glan_ref[...]) * jax.nn.silu(rg_ref[:, cs])).astype(BF16)
    ya = _dot(ya_ref[...], wdo_ref[...])
    yg = _dot(yg_ref[...], wgo_ref[...])
    mix = gate[:, 0:D_MODEL] * ya + gate[:, D_MODEL:2 * D_MODEL] * yg
    o_ref[...] = x + _dot(mix.astype(BF16), wout_ref[...])


def _merge(x, oa, og, rg, l, p, lam_init):
    t = x.shape[0]
    row = lambda w: pl.BlockSpec((TM, w), lambda i: (i, 0))
    lay2 = lambda w: pl.BlockSpec((None, 1, w), lambda i: (l, 0, 0))
    lay3 = lambda a, b: pl.BlockSpec((None, a, b), lambda i: (l, 0, 0), pipeline_mode=pl.Buffered(1))
    return pl.pallas_call(
        functools.partial(_merge_kernel, lam_init=lam_init),
        grid=(t // TM,),
        in_specs=[row(D_MODEL), row(WA), row(WGV), row(WGV), lay2(D_MODEL), lay2(2 * DHA), lay2(DVG),
                  lay3(2 * D_MODEL, D_MODEL), lay3(WA, D_MODEL), lay3(WGV, D_MODEL), lay3(D_MODEL, D_MODEL)],
        out_specs=row(D_MODEL),
        out_shape=jax.ShapeDtypeStruct((t, D_MODEL), F32),
        scratch_shapes=[pltpu.VMEM((TM, WA), BF16), pltpu.VMEM((TM, WGV), BF16)],
        compiler_params=_cparams(("parallel",)),
        name="merge",
    )(x, oa, og, rg, p["norm_mix"], p["subln"], p["gla_norm"],
      p["w_gate"], p["w_diff_out"], p["w_gla_out"], p["w_out"])


def _ffn_kernel(x_ref, g_ref, wg_ref, wu_ref, wd_ref, o_ref, *, chunk):
    x = x_ref[...]
    h = _rms(x, g_ref[...]).astype(BF16)
    acc = x
    for c0 in range(0, wg_ref.shape[1], chunk):
        a = jax.nn.silu(_dot(h, wg_ref[:, c0:c0 + chunk])) * _dot(h, wu_ref[:, c0:c0 + chunk])
        acc = acc + _dot(a.astype(BF16), wd_ref[c0:c0 + chunk, :])
    o_ref[...] = acc


def _ffn(x, l, p):
    t = x.shape[0]
    f = p["w_ffn_gate"].shape[2]
    row = pl.BlockSpec((TM, D_MODEL), lambda i: (i, 0))
    once = pl.Buffered(1)
    return pl.pallas_call(
        functools.partial(_ffn_kernel, chunk=f // 2),
        grid=(t // TM,),
        in_specs=[row, pl.BlockSpec((None, 1, D_MODEL), lambda i: (l, 0, 0)),
                  pl.BlockSpec((None, D_MODEL, f), lambda i: (l, 0, 0), pipeline_mode=once),
                  pl.BlockSpec((None, D_MODEL, f), lambda i: (l, 0, 0), pipeline_mode=once),
                  pl.BlockSpec((None, f, D_MODEL), lambda i: (l, 0, 0), pipeline_mode=once)],
        out_specs=row,
        out_shape=jax.ShapeDtypeStruct((t, D_MODEL), F32),
        compiler_params=_cparams(("parallel",)),
        name="ffn",
    )(x, p["norm_ffn"], p["w_ffn_gate"], p["w_ffn_up"], p["w_ffn_down"])


def kernel(x_prompt, x_sample, cache_k, cache_v, state_gla, page_table, rel_bias, norm_mix, w_in, w_alpha2, b_alpha, qn_gain, kn_gain, lam_q1, lam_k1, lam_q2, lam_k2, subln_gain, gla_norm_gain, w_diff_out, w_gla_out, w_out, norm_ffn, w_ffn_gate, w_ffn_up, w_ffn_down):
    batch, seq, _ = x_prompt.shape
    dec_batch, ts, _ = x_sample.shape
    n_pages = page_table.shape[1]
    n_pool = cache_k.shape[1]
    past = n_pages * PAGE_SIZE
    depth = w_in.shape[0]
    gate0 = MAIN_W + GATE_RANK

    vec = lambda a: a.reshape(depth, 1, a.shape[-1])
    w_in_t = jnp.transpose(w_in, (0, 2, 1)).astype(BF16)
    p = {
        "norm_mix": vec(norm_mix), "norm_ffn": vec(norm_ffn), "b_alpha": vec(b_alpha),
        "gq": vec(jnp.tile(qn_gain, (1, WA // DHA))), "gk": vec(jnp.tile(kn_gain, (1, WA // DHA))),
        "subln": vec(subln_gain), "gla_norm": vec(gla_norm_gain),
        "w_in_t": w_in_t,
        "w_gate": w_in_t[:, gate0:, :],
        "w_alpha2": w_alpha2.astype(BF16),
        "w_diff_out": w_diff_out.astype(BF16), "w_gla_out": w_gla_out.astype(BF16), "w_out": w_out.astype(BF16),
        "w_ffn_gate": w_ffn_gate.astype(BF16), "w_ffn_up": w_ffn_up.astype(BF16),
        "w_ffn_down": w_ffn_down.astype(BF16),
        "gsum": jnp.kron(jnp.eye(WA // DHA, dtype=F32), jnp.full((DHA, DHA), 1.0 / DHA, F32)).astype(BF16),
    }
    lamp = jnp.stack([lam_q1, lam_k1, lam_q2, lam_k2], axis=1)

    ckt = jnp.transpose(cache_k, (0, 1, 3, 4, 5, 2)).reshape(depth, n_pool, WA, PAGE_SIZE)
    cv4 = cache_v.reshape(depth, n_pool, PAGE_SIZE * HA, LANES)
    sg = state_gla.reshape(depth, dec_batch, HG // 2, 2 * DKG, DVG)
    s0p = jnp.zeros((1, batch, HG // 2, 2 * DKG, DVG), F32)

    bias_p = _bias_tables(rel_bias, 3, TK, TQ, TQ, 0, TK, True, True)
    bias_s = _bias_tables(rel_bias, 1, 2 * ts, past + PAGE_SIZE, ts, past, 0, False, False)
    bias_s = bias_s.reshape(HA * 2 * ts, past + PAGE_SIZE)
    flags = _logit_bound_flags(qn_gain, kn_gain, rel_bias)

    xp = x_prompt.reshape(batch * seq, D_MODEL)
    xs = x_sample.reshape(dec_batch * ts, D_MODEL)
    kv = ()
    kv_s = ()
    sp_all = ss_all = None
    for l in range(depth):
        lam_init = 0.8 - 0.6 * math.exp(-0.3 * l)
        qa, kt_all, v4_all, kb, vtb, qg, kg, vg, rg, la = _inproj(xp, l, p, True, kv, batch, seq)
        kv = (kt_all, v4_all)
        oa = _attn_prompt(qa, kb, vtb, bias_p, flags, lamp[l], l, lam_init, batch, seq)
        og, sp_all = _gla(qg, kg, vg, la, s0p, 0, sp_all, l, depth, batch, seq, batch, GLA_C, GLA_C)
        xp = _merge(xp, oa, og, rg, l, p, lam_init)
        xp = _ffn(xp, l, p)
        qa, ka, va, kst_all, vs4_all, qg, kg, vg, rg, la = _inproj(xs, l, p, False, kv_s, dec_batch, ts)
        kv_s = (kst_all, vs4_all)
        oa = _attn_sample(qa, ka, va, ckt, cv4, page_table, bias_s, lamp[l], l, lam_init, dec_batch, ts)
        og, ss_all = _gla(qg, kg, vg, la, sg, l, ss_all, l, depth, dec_batch, ts, GLA_SB, ts, 2 * ts)
        xs = _merge(xs, oa, og, rg, l, p, lam_init)
        xs = _ffn(xs, l, p)
    kt_all, v4_all = kv
    kst_all, vs4_all = kv_s
    k_prompt = jnp.transpose(kt_all.reshape(depth, batch, HA, 2, DHA, seq), (0, 1, 5, 2, 3, 4))
    v_prompt = v4_all.reshape(depth, batch, seq, HA, 2 * DHA)
    k_sample = jnp.transpose(kst_all.reshape(depth, ts, HA, 2, DHA, dec_batch), (0, 5, 1, 2, 3, 4))
    v_sample = vs4_all.reshape(depth, dec_batch, ts, HA, 2 * DHA)
    return (xp.reshape(batch, seq, D_MODEL), xs.reshape(dec_batch, ts, D_MODEL),
            k_prompt, v_prompt, sp_all.reshape(depth, batch, HG, DKG, DVG),
            k_sample, v_sample, ss_all.reshape(depth, dec_batch, HG, DKG, DVG))
```

```python
import functools
import math

import jax
import jax.numpy as jnp
from jax import lax
from jax.experimental import pallas as pl
from jax.experimental.pallas import tpu as pltpu

F32 = jnp.float32
BF16 = jnp.bfloat16

D_MODEL = 1024
HA = 4
DHA = 64
HG = 4
DKG = 64
DVG = 128
GATE_RANK = 16
GATE_NORM = 16.0
N_BUCKETS = 32
MAX_DISTANCE = 128
PAGE_SIZE = 128
EPS = 1e-6
WA = HA * 2 * DHA
WGK = HG * DKG
WGV = HG * DVG
MAIN_W = 3 * WA + 2 * WGK + 2 * WGV
LANES = 128
NEG = -1e30
LOG2E = math.log2(math.e)
GLA_RANGE_SAFE = 80.0
LOGIT_SAFE = 60.0

TM = 512
TQ = 512
TK = 512
GLA_C = 64
GLA_SB = 8
VMEM_LIMIT = 56 * 1024 * 1024


def _cparams(sem):
    return pltpu.CompilerParams(dimension_semantics=sem, vmem_limit_bytes=VMEM_LIMIT)


def _rms(x, gain):
    ms = jnp.mean(x * x, axis=-1, keepdims=True)
    return x * lax.rsqrt(ms + EPS) * gain


def _dot(a, b):
    return jnp.dot(a, b, preferred_element_type=F32)


def _dot_nt(a, b):
    return lax.dot_general(a, b, (((1,), (1,)), ((), ())), preferred_element_type=F32)


def _dot_tn(a, b):
    return lax.dot_general(a, b, (((0,), (0,)), ((), ())), preferred_element_type=F32)


def _lam(lamp_ref, lam_init):
    a = jnp.sum(lamp_ref[0:1, :] * lamp_ref[1:2, :], axis=-1, keepdims=True)
    b = jnp.sum(lamp_ref[2:3, :] * lamp_ref[3:4, :], axis=-1, keepdims=True)
    return jnp.exp(a) - jnp.exp(b) + lam_init


def _bias_kernel(rb_ref, o_ref, *, period, off0, off_step, transposed, shifted):
    h = pl.program_id(0)
    t = pl.program_id(1)
    rows, cols = o_ref.shape
    r = lax.broadcasted_iota(jnp.int32, (rows, cols), 0)
    c = lax.broadcasted_iota(jnp.int32, (rows, cols), 1)
    if transposed:
        r, c = c, r
    d = off0 + t * off_step + jnp.bitwise_and(r, period - 1) - c
    n = jnp.maximum(d, 0)
    max_exact = N_BUCKETS // 2
    nf = jnp.maximum(n, 1).astype(F32)
    large = max_exact + (jnp.log(nf / max_exact) / math.log(MAX_DISTANCE / max_exact)
                         * (N_BUCKETS - max_exact)).astype(jnp.int32)
    large = jnp.minimum(large, N_BUCKETS - 1)
    bucket = jnp.where(n < max_exact, n, large)
    val = jnp.zeros((rows, cols), F32)
    for k in range(N_BUCKETS):
        val = jnp.where(bucket == k, rb_ref[k, h], val)
    if shifted:
        val = val - rb_ref[N_BUCKETS - 1, h]
    o_ref[...] = jnp.where(d >= 0, val * LOG2E, NEG)


def _bias_tables(rel_bias, n_t, rows, cols, period, off0, off_step, transposed, shifted):
    return pl.pallas_call(
        functools.partial(_bias_kernel, period=period, off0=off0, off_step=off_step,
                          transposed=transposed, shifted=shifted),
        grid=(HA, n_t),
        in_specs=[pl.BlockSpec(memory_space=pltpu.SMEM)],
        out_specs=pl.BlockSpec((None, None, rows, cols), lambda h, t: (h, t, 0, 0)),
        out_shape=jax.ShapeDtypeStruct((HA, n_t, rows, cols), F32),
        compiler_params=_cparams(("arbitrary", "arbitrary")),
        name="bias_tables",
    )(rel_bias)


def _inproj_kernel(*refs, prompt, n_alias, ts):
    (x_ref, gmix_ref, w_ref, walr_ref, wa2_ref, ba_ref, gq_ref, gk_ref, gsum_ref) = refs[:9]
    outs = refs[9 + n_alias:]
    if prompt:
        (qa_ref, ka_ref, va_ref, kb_ref, vtb_ref, qg_ref, kg_ref, vg_ref, rg_ref, la_ref) = outs
    else:
        (qa_ref, ka_ref, va_ref, kst_ref, vs4_ref, qg_ref, kg_ref, vg_ref, rg_ref, la_ref, kscr_ref) = outs
    h = _rms(x_ref[...], gmix_ref[...]).astype(BF16)
    proj = _dot_nt(h, w_ref[...])
    gs = gsum_ref[...]
    qa = proj[:, 0:WA]
    ka = proj[:, WA:2 * WA]
    qms = _dot((qa * qa).astype(BF16), gs)
    kms = _dot((ka * ka).astype(BF16), gs)
    qan = (qa * lax.rsqrt(qms + EPS) * gq_ref[...] * (DHA ** -0.5 * LOG2E)).astype(BF16)
    if prompt:
        for hd in range(HA):
            qa_ref[hd] = qan[:, hd * LANES:(hd + 1) * LANES]
    else:
        qa_ref[...] = qan
    kan = ka * lax.rsqrt(kms + EPS) * gk_ref[...]
    if prompt:
        ka_ref[...] = kan.T
        for hd in range(HA):
            kb_ref[hd] = kan[:, hd * LANES:(hd + 1) * LANES].astype(BF16)
        vtb_ref[...] = proj[:, 2 * WA:3 * WA].T.astype(BF16)
        for hd in range(HA):
            va_ref[pl.ds(hd, TM, stride=HA), :] = proj[:, 2 * WA + hd * LANES:2 * WA + (hd + 1) * LANES]
    else:
        ka_ref[...] = kan
        va_ref[...] = proj[:, 2 * WA:3 * WA]
        for hd in range(HA):
            vs4_ref[pl.ds(hd, TM, stride=HA), :] = proj[:, 2 * WA + hd * LANES:2 * WA + (hd + 1) * LANES]
        for j in range(WA // LANES):
            kscr_ref[j] = kan[:, j * LANES:(j + 1) * LANES]
        nbt = TM // ts
        for tile in range(kst_ref.shape[2] // nbt):
            @pl.when(pl.program_id(0) == tile)
            def _(tile=tile):
                lo, hi = tile * nbt, (tile + 1) * nbt
                for t in range(ts):
                    for j in range(WA // LANES):
                        rows = kscr_ref[j, pl.ds(t, nbt, stride=ts), :]
                        parts = [jnp.zeros((n, LANES), F32) for n in (lo,) if n] + [rows]
                        parts += [jnp.zeros((n, LANES), F32) for n in (LANES - hi,) if n]
                        tr = jnp.concatenate(parts, axis=0).T
                        kst_ref[t, j * LANES:(j + 1) * LANES, lo:hi] = tr[:, lo:hi]
    o = 3 * WA
    qg_ref[...] = proj[:, o:o + WGK] * (DKG ** -0.5)
    kg_ref[...] = proj[:, o + WGK:o + 2 * WGK]
    o += 2 * WGK
    vg_ref[...] = proj[:, o:o + WGV]
    rg_ref[...] = proj[:, o + WGV:o + 2 * WGV]
    alr = _dot_nt(h, walr_ref[...])
    z = _dot(alr.astype(BF16), wa2_ref[...]) + ba_ref[...]
    log_sig = jnp.minimum(z, 0.0) - jnp.log(1.0 + jnp.exp(-jnp.abs(z)))
    la_ref[...] = log_sig * (1.0 / GATE_NORM)


def _inproj(x, l, p, prompt, kv_prev, batch, seq):
    t = x.shape[0]
    depth = p["w_in_t"].shape[0]
    row = lambda w: pl.BlockSpec((TM, w), lambda i: (i, 0))
    lay2 = lambda w: pl.BlockSpec((None, 1, w), lambda i: (l, 0, 0))
    lay3 = lambda a, b: pl.BlockSpec((None, a, b), lambda i: (l, 0, 0), pipeline_mode=pl.Buffered(1))
    in_specs = [row(D_MODEL), lay2(D_MODEL), lay3(MAIN_W, D_MODEL),
                pl.BlockSpec((None, GATE_RANK, D_MODEL), lambda i: (l, MAIN_W // GATE_RANK, 0)),
                lay3(GATE_RANK, WGK), lay2(WGK), lay2(WA), lay2(WA),
                pl.BlockSpec((WA, WA), lambda i: (0, 0))]
    args = [x, p["norm_mix"], p["w_in_t"], p["w_in_t"], p["w_alpha2"], p["b_alpha"], p["gq"], p["gk"], p["gsum"]]
    rest = [(WGK, F32), (WGK, F32), (WGV, F32), (WGV, F32), (WGK, F32)]
    if prompt:
        spb = seq // TM
        heads = pl.BlockSpec((None, HA, TM, LANES), lambda i: (i // spb, 0, i % spb, 0))
        heads_shape = jax.ShapeDtypeStruct((batch, HA, seq, LANES), BF16)
        kv_specs = [pl.BlockSpec((None, None, WA, TM), lambda i: (l, i // spb, 0, i % spb)),
                    pl.BlockSpec((None, TM * HA, LANES), lambda i: (l, i, 0)),
                    heads,
                    pl.BlockSpec((None, WA, TM), lambda i: (i // spb, 0, i % spb))]
        kv_shapes = [jax.ShapeDtypeStruct((depth, batch, WA, seq), F32),
                     jax.ShapeDtypeStruct((depth, t * HA, LANES), F32),
                     heads_shape,
                     jax.ShapeDtypeStruct((batch, WA, seq), BF16)]
        q_spec, q_shape = heads, heads_shape
        first_kv = 1
        scratch = []
    else:
        kv_specs = [row(WA), row(WA),
                    pl.BlockSpec((None, seq, WA, batch), lambda i: (l, 0, 0, 0)),
                    pl.BlockSpec((None, TM * HA, LANES), lambda i: (l, i, 0))]
        kv_shapes = [jax.ShapeDtypeStruct((t, WA), F32)] * 2 + [
            jax.ShapeDtypeStruct((depth, seq, WA, batch), F32),
            jax.ShapeDtypeStruct((depth, t * HA, LANES), F32)]
        q_spec, q_shape = row(WA), jax.ShapeDtypeStruct((t, WA), BF16)
        first_kv = 3
        scratch = [pltpu.VMEM((WA // LANES, TM, LANES), F32)]
    aliases = {}
    for n, a in enumerate(kv_prev):
        in_specs.append(pl.BlockSpec(memory_space=pl.ANY))
        args.append(a)
        aliases[9 + n] = first_kv + n
    return pl.pallas_call(
        functools.partial(_inproj_kernel, prompt=prompt, n_alias=len(aliases), ts=seq),
        grid=(t // TM,),
        in_specs=in_specs,
        out_specs=[q_spec] + kv_specs + [row(w) for w, _ in rest],
        out_shape=[q_shape] + kv_shapes
                  + [jax.ShapeDtypeStruct((t, w), dt) for w, dt in rest],
        scratch_shapes=scratch,
        input_output_aliases=aliases,
        compiler_params=_cparams(("parallel",) if prompt else ("arbitrary",)),
        name="inproj",
    )(*args)


def _bound_kernel(gq_ref, gk_ref, rb_ref, o_ref):
    mq = jnp.max(jnp.abs(gq_ref[...]), axis=1, keepdims=True)
    mk = jnp.max(jnp.abs(gk_ref[...]), axis=1, keepdims=True)
    rb = rb_ref[...]
    shifted = jnp.abs(rb - rb[N_BUCKETS - 1:N_BUCKETS, :])
    bmax = jnp.max(jnp.max(shifted, axis=1, keepdims=True), axis=0, keepdims=True)
    bound = LOG2E * (1.01 * DHA ** 0.5 * mq * mk + bmax)
    o_ref[...] = jnp.broadcast_to((bound <= LOGIT_SAFE).astype(jnp.int32), o_ref.shape)


def _logit_bound_flags(qn_gain, kn_gain, rel_bias):
    depth = qn_gain.shape[0]
    return pl.pallas_call(
        _bound_kernel,
        out_shape=jax.ShapeDtypeStruct((depth, LANES), jnp.int32),
        name="logit_bound",
    )(qn_gain, kn_gain, rel_bias)


def _attn_kernel(flag_ref, lamp_ref, q_ref, kb_ref, vtb_ref, bias_ref, o_ref,
                 qs_ref, m_ref, l_ref, acc_ref, sa_ref, sb_ref, *, lam_init, seq, layer):
    nq = seq // TQ
    bounded = flag_ref[layer, 0] == 1

    lane = lax.broadcasted_iota(jnp.int32, (1, LANES), 1)
    lam = _lam(lamp_ref, lam_init)

    def logits(j, dst_ref):
        start = pl.multiple_of(j * TK, TK)
        dst_ref[...] = _dot_nt(kb_ref[pl.ds(start, TK), :], qs_ref[...])

    def start_block(i):
        q = q_ref[pl.ds(pl.multiple_of(i * TQ, TQ), TQ), :]
        qs_ref[0:TQ, :] = jnp.where(lane < DHA, q, jnp.zeros_like(q))
        qs_ref[TQ:2 * TQ, :] = jnp.where(lane >= DHA, q, jnp.zeros_like(q))
        m_ref[...] = jnp.full(m_ref.shape, NEG, F32)
        l_ref[...] = jnp.zeros(l_ref.shape, F32)
        acc_ref[...] = jnp.zeros(acc_ref.shape, F32)
        logits(jnp.where(bounded, i, 0), sa_ref)

    start_block(0)


    def biased(src_ref, table):
        b = bias_ref[table]
        return jnp.concatenate([src_ref[:, 0:TQ] + b, src_ref[:, TQ:2 * TQ] + b], axis=1)

    def accumulate(p, j):
        start = pl.multiple_of(j * TK, TK)
        l_ref[...] += jnp.sum(p, axis=0, keepdims=True)
        acc_ref[...] += _dot(vtb_ref[:, pl.ds(start, TK)], p.astype(BF16))

    def q_block(i, carry):
        jl = i

        @pl.when(bounded)
        def _():
            @pl.when(jl == 0)
            def _():
                accumulate(jnp.exp2(biased(sa_ref, 0)), jl)

            @pl.when(jl >= 1)
            def _():
                logits(jl - 1, sb_ref)
                accumulate(jnp.exp2(biased(sa_ref, 0)), jl)
                logits(0, sa_ref)
                accumulate(jnp.exp2(biased(sb_ref, 1)), jl - 1)

            n_far = jnp.maximum(jl - 1, 0)

            def far_pair(jj, carry):
                j0 = 2 * jj
                logits(jnp.minimum(j0 + 1, jl), sb_ref)
                accumulate(jnp.exp2(sa_ref[...]), j0)

                @pl.when(j0 + 1 < n_far)
                def _():
                    logits(jnp.minimum(j0 + 2, jl), sa_ref)
                    accumulate(jnp.exp2(sb_ref[...]), j0 + 1)

                return carry

            lax.fori_loop(0, (n_far + 1) // 2, far_pair, 0)

        @pl.when(jnp.logical_not(bounded))
        def _():
            def update(src_ref, j):
                table = jnp.minimum(jl - j, 2)
                start = pl.multiple_of(j * TK, TK)
                s = biased(src_ref, table)
                m_prev = m_ref[...]
                m_new = jnp.maximum(m_prev, jnp.max(s, axis=0, keepdims=True))
                alpha = jnp.exp2(m_prev - m_new)
                p = jnp.exp2(s - m_new)
                l_ref[...] = alpha * l_ref[...] + jnp.sum(p, axis=0, keepdims=True)
                acc_ref[...] = alpha * acc_ref[...] + _dot(vtb_ref[:, pl.ds(start, TK)], p.astype(BF16))
                m_ref[...] = m_new

            def pair_body(jj, carry):
                j0 = 2 * jj
                logits(jnp.minimum(j0 + 1, jl), sb_ref)
                update(sa_ref, j0)

                @pl.when(j0 + 1 <= jl)
                def _():
                    logits(jnp.minimum(j0 + 2, jl), sa_ref)
                    update(sb_ref, j0 + 1)

                return carry

            lax.fori_loop(0, jl // 2 + 1, pair_body, 0)

        o = acc_ref[...] * (1.0 / l_ref[...])
        o_ref[pl.ds(pl.multiple_of(i * TQ, TQ), TQ), :] = (o[:, 0:TQ] - lam * o[:, TQ:2 * TQ]).T
        start_block(jnp.minimum(i + 1, nq - 1))
        return carry

    lax.fori_loop(0, nq, q_block, 0)


def _attn_prompt(qa, kb, vtb, bias, flags, lamp, l, lam_init, batch, seq):
    blk = pl.BlockSpec((None, seq, LANES), lambda b, h: (b, 0, h))
    head_blk = pl.BlockSpec((None, None, seq, LANES), lambda b, h: (b, h, 0, 0))
    out = pl.pallas_call(
        functools.partial(_attn_kernel, lam_init=lam_init, seq=seq, layer=l),
        grid=(batch, HA),
        in_specs=[pl.BlockSpec(memory_space=pltpu.SMEM),
                  pl.BlockSpec((4, DHA), lambda b, h: (0, 0)),
                  head_blk, head_blk,
                  pl.BlockSpec((None, LANES, seq), lambda b, h: (b, h, 0)),
                  pl.BlockSpec((None, 3, TK, TQ), lambda b, h: (h, 0, 0, 0))],
        out_specs=head_blk,
        out_shape=jax.ShapeDtypeStruct((batch, HA, seq, LANES), F32),
        scratch_shapes=[pltpu.VMEM((2 * TQ, LANES), BF16),
                        pltpu.VMEM((1, 2 * TQ), F32), pltpu.VMEM((1, 2 * TQ), F32),
                        pltpu.VMEM((LANES, 2 * TQ), F32),
                        pltpu.VMEM((TK, 2 * TQ), F32), pltpu.VMEM((TK, 2 * TQ), F32)],
        compiler_params=_cparams(("parallel", "parallel")),
        name="attn_prompt",
    )(flags, lamp, qa, kb, vtb, bias)
    return out


def _attn_sample_kernel(pt_ref, lamp_ref, q_ref, kn_ref, vn_ref, bias_ref, *rest, n_pages, ts, lam_init):
    kp = rest[:n_pages]
    vp = rest[n_pages:2 * n_pages]
    o_ref = rest[2 * n_pages]
    s_ref = rest[2 * n_pages + 1]
    nr = HA * 2 * ts
    past = n_pages * PAGE_SIZE

    q = q_ref[...].astype(F32)
    qt = jnp.concatenate([q] * (2 * HA), axis=0)
    r = lax.broadcasted_iota(jnp.int32, (nr, WA), 0)
    c = lax.broadcasted_iota(jnp.int32, (nr, WA), 1)
    keep = lax.shift_right_logical(c, int(math.log2(DHA))) == lax.shift_right_logical(r, int(math.log2(ts)))
    qbd = jnp.where(keep, qt, 0.0).astype(BF16)
    zpad = jnp.zeros((PAGE_SIZE - ts, WA), F32)
    knp = jnp.concatenate([kn_ref[...], zpad], axis=0).astype(BF16)
    vnp = jnp.concatenate([vn_ref[...], zpad], axis=0).astype(BF16)

    for pg in range(n_pages):
        s_ref[:, pg * PAGE_SIZE:(pg + 1) * PAGE_SIZE] = _dot(qbd, kp[pg][...].astype(BF16))
    s_ref[:, past:past + PAGE_SIZE] = _dot_nt(qbd, knp)

    s = s_ref[...] + bias_ref[...]
    m = jnp.max(s, axis=-1, keepdims=True)
    p = jnp.exp2(s - m)
    l = jnp.sum(p, axis=-1, keepdims=True)
    pb = p.astype(BF16)
    lam = _lam(lamp_ref, lam_init)
    for h in range(HA):
        rs = slice(h * 2 * ts, (h + 1) * 2 * ts)
        cs = slice(h * LANES, (h + 1) * LANES)
        acc = _dot(pb[rs, past:past + PAGE_SIZE], vnp[:, cs])
        for pg in range(n_pages):
            vh = vp[pg][pl.ds(h, PAGE_SIZE, stride=HA), :].astype(BF16)
            acc = acc + _dot(pb[rs, pg * PAGE_SIZE:(pg + 1) * PAGE_SIZE], vh)
        acc = acc / l[rs]
        o_ref[:, cs] = acc[0:ts, :] - lam * acc[ts:2 * ts, :]


def _attn_sample(qa, ka, va, ckt, cv4, page_table, bias, lamp, l, lam_init, dec_batch, ts):
    n_pages = page_table.shape[1]
    nr = HA * 2 * ts
    width = n_pages * PAGE_SIZE + PAGE_SIZE
    tok = pl.BlockSpec((None, ts, WA), lambda b, pt: (b, 0, 0))
    page = lambda pg: pl.BlockSpec((None, None, HA * PAGE_SIZE, LANES), lambda b, pt: (l, pt[b, pg], 0, 0))
    grid_spec = pltpu.PrefetchScalarGridSpec(
        num_scalar_prefetch=1,
        grid=(dec_batch,),
        in_specs=[pl.BlockSpec((4, DHA), lambda b, pt: (0, 0)), tok, tok, tok,
                  pl.BlockSpec((nr, width), lambda b, pt: (0, 0))]
                 + [page(pg) for pg in range(n_pages)] * 2,
        out_specs=tok,
        scratch_shapes=[pltpu.VMEM((nr, width), F32)],
    )
    out = pl.pallas_call(
        functools.partial(_attn_sample_kernel, n_pages=n_pages, ts=ts, lam_init=lam_init),
        grid_spec=grid_spec,
        out_shape=jax.ShapeDtypeStruct((dec_batch, ts, WA), F32),
        compiler_params=_cparams(("parallel",)),
        name="attn_sample",
    )(page_table, lamp, qa.reshape(dec_batch, ts, WA), ka.reshape(dec_batch, ts, WA),
      va.reshape(dec_batch, ts, WA), bias, *([ckt] * n_pages), *([cv4] * n_pages))
    return out.reshape(dec_batch * ts, WA)


def _split3(x):
    x1 = x.astype(BF16)
    r1 = x - x1.astype(F32)
    x2 = r1.astype(BF16)
    x3 = (r1 - x2.astype(F32)).astype(BF16)
    return x1, x2, x3


def _gla_kernel(q_ref, k_ref, v_ref, g_ref, s0_ref, tril_ref, *rest, nb, c_in, c):
    o_ref, sout_ref, st_scr = rest[-3:]
    ci = pl.program_id(1)
    units = [(b, pr) for b in range(nb) for pr in range(HG // 2)]

    @pl.when(ci == 0)
    def _():
        for b, pr in units:
            st_scr[b, pr] = s0_ref[b, pr].T

    def pad(x):
        if c_in == c:
            return x
        return jnp.concatenate([x, jnp.zeros((c - c_in, x.shape[1]), x.dtype)], axis=0)

    lane = lax.broadcasted_iota(jnp.int32, (1, LANES), 1)
    head0 = lane < DKG
    tt = lax.broadcasted_iota(jnp.int32, (2 * c, c), 0)
    ss = lax.broadcasted_iota(jnp.int32, (2 * c, c), 1)
    causal = jnp.bitwise_and(tt, c - 1) >= ss
    mid = c // 2 - 1

    def heads_on_rows(x):
        return jnp.concatenate([jnp.where(head0, x, 0.0), jnp.where(head0, 0.0, x)], axis=0).astype(BF16)

    g_all = jnp.concatenate([pad(g_ref[b]) for b in range(nb)], axis=1)
    tril = tril_ref[...]
    g3 = _split3(g_all)
    bc_all = _dot(tril, g3[0]) + _dot(tril, g3[1]) + _dot(tril, g3[2])

    chunk_range = jnp.max(-bc_all[c - 1:c, :])

    @pl.when(chunk_range <= GLA_RANGE_SAFE)
    def _():
        prep = []
        for b, pr in units:
            ks = slice(pr * LANES, (pr + 1) * LANES)
            q = pad(q_ref[b, :, ks])
            k = pad(k_ref[b, :, ks])
            bcum = bc_all[:, b * WGK + pr * LANES:b * WGK + (pr + 1) * LANES]
            bm = bcum[mid:mid + 1, :]
            bl = bcum[c - 1:c, :]
            prep.append(dict(
                qt=heads_on_rows(q * jnp.exp(bcum - bm)),
                qi=heads_on_rows(q * jnp.exp(bcum)),
                kt=(k * jnp.exp(bm - bcum)).astype(BF16),
                kd=(k * jnp.exp(bl - bcum)).astype(BF16),
                decay=jnp.exp(bl),
                v2=pad(v_ref[b, :, 2 * pr * DVG:2 * (pr + 1) * DVG]).astype(BF16)))

        amat = [jnp.where(causal, _dot_nt(u["qt"], u["kt"]), 0.0).astype(BF16) for u in prep]
        inter = [_dot_nt(u["qi"], st_scr[b, pr].astype(BF16)) for u, (b, pr) in zip(prep, units)]
        for u, a, it, (b, pr) in zip(prep, amat, inter, units):
            for hh in range(2):
                o = (_dot(a[hh * c:(hh + 1) * c, :], u["v2"][:, hh * DVG:(hh + 1) * DVG])
                     + it[hh * c:(hh + 1) * c, :])
                o_ref[b, :, (2 * pr + hh) * DVG:(2 * pr + hh + 1) * DVG] = o[0:c_in, :]
        for u, (b, pr) in zip(prep, units):
            upd = _dot_tn(u["v2"], u["kd"])
            st_scr[b, pr] = st_scr[b, pr] * u["decay"] + jnp.where(head0, upd[0:DVG, :], upd[DVG:2 * DVG, :])

    @pl.when(jnp.logical_not(chunk_range <= GLA_RANGE_SAFE))
    def _():
        row_head0 = lax.broadcasted_iota(jnp.int32, (LANES, 1), 0) < DKG

        def columns(x):
            return jnp.concatenate([x, jnp.zeros((LANES - c_in, LANES), F32)], axis=0).T

        def unit_body(u, carry):
            b = u // (HG // 2)
            pr = u % (HG // 2)
            ks = pl.ds(pl.multiple_of(pr * LANES, LANES), LANES)
            qc = columns(q_ref[b, :, ks])
            kc = columns(k_ref[b, :, ks])
            ac = columns(jnp.exp(g_ref[b, :, ks]))
            v2 = v_ref[b, :, pl.ds(pl.multiple_of(pr * 2 * DVG, 2 * DVG), 2 * DVG)]
            s = st_scr[b, pr].T
            for t in range(c_in):
                vrow = jnp.where(row_head0, v2[t:t + 1, 0:DVG], v2[t:t + 1, DVG:2 * DVG])
                s = s * ac[:, t:t + 1] + kc[:, t:t + 1] * vrow
                w = qc[:, t:t + 1] * s
                o_ref[b, pl.ds(t, 1), pl.ds(pl.multiple_of(pr * 2 * DVG, 2 * DVG), DVG)] = (
                    jnp.sum(w[0:DKG, :], axis=0, keepdims=True))
                o_ref[b, pl.ds(t, 1), pl.ds(pl.multiple_of(pr * 2 * DVG + DVG, DVG), DVG)] = (
                    jnp.sum(w[DKG:2 * DKG, :], axis=0, keepdims=True))
            st_scr[b, pr] = s.T
            return carry

        lax.fori_loop(0, len(units), unit_body, 0)

    @pl.when(ci == pl.num_programs(1) - 1)
    def _():
        for b, pr in units:
            sout_ref[b, pr] = st_scr[b, pr].T


def _gla(qg, kg, vg, la, s0, s0_layer, s_prev, layer, depth, batch, seq, nb, c_in, c):
    n_chunks = seq // c_in
    tril = jnp.tril(jnp.ones((c, c), BF16))
    tok = lambda w: pl.BlockSpec((nb, c_in, w), lambda bi, ci: (bi, ci, 0))
    st_out = pl.BlockSpec((None, nb, HG // 2, 2 * DKG, DVG), lambda bi, ci: (layer, bi, 0, 0, 0))
    st_in = pl.BlockSpec((None, nb, HG // 2, 2 * DKG, DVG), lambda bi, ci: (s0_layer, bi, 0, 0, 0))
    in_specs = [tok(WGK), tok(WGK), tok(WGV), tok(WGK), st_in, pl.BlockSpec((c, c), lambda bi, ci: (0, 0))]
    args = [qg.reshape(batch, seq, WGK), kg.reshape(batch, seq, WGK), vg.reshape(batch, seq, WGV),
            la.reshape(batch, seq, WGK), s0, tril]
    aliases = {}
    if s_prev is not None:
        in_specs.append(pl.BlockSpec(memory_space=pl.ANY))
        args.append(s_prev)
        aliases = {6: 1}
    o, s_all = pl.pallas_call(
        functools.partial(_gla_kernel, nb=nb, c_in=c_in, c=c),
        grid=(batch // nb, n_chunks),
        in_specs=in_specs,
        out_specs=[tok(WGV), st_out],
        out_shape=[jax.ShapeDtypeStruct((batch, seq, WGV), F32),
                   jax.ShapeDtypeStruct((depth, batch, HG // 2, 2 * DKG, DVG), F32)],
        scratch_shapes=[pltpu.VMEM((nb, HG // 2, 2 * DKG, DVG), F32)],
        input_output_aliases=aliases,
        compiler_params=_cparams(("parallel", "arbitrary")),
        name="gla",
    )(*args)
    return o.reshape(batch * seq, WGV), s_all


def _merge_kernel(x_ref, oa_ref, og_ref, rg_ref, gmix_ref, subln_ref, glan_ref,
                  wgate_ref, wdo_ref, wgo_ref, wout_ref, o_ref, ya_ref, yg_ref, *, lam_init):
    x = x_ref[...]
    h = _rms(x, gmix_ref[...]).astype(BF16)
    gate = jax.nn.sigmoid(_dot_nt(h, wgate_ref[...]))
    for hd in range(HA):
        cs = slice(hd * LANES, (hd + 1) * LANES)
        oa = oa_ref[hd] if len(oa_ref.shape) == 3 else oa_ref[:, cs]
        ya_ref[:, cs] = (_rms(oa, subln_ref[...]) * (1.0 - lam_init)).astype(BF16)
        yg_ref[:, cs] = (_rms(og_ref[:, cs], glan_ref[...]) * jax.nn.silu(rg_ref[:, cs])).astype(BF16)
    ya = _dot(ya_ref[...], wdo_ref[...])
    yg = _dot(yg_ref[...], wgo_ref[...])
    mix = gate[:, 0:D_MODEL] * ya + gate[:, D_MODEL:2 * D_MODEL] * yg
    o_ref[...] = x + _dot(mix.astype(BF16), wout_ref[...])


def _merge(x, oa, og, rg, l, p, lam_init):
    t = x.shape[0]
    row = lambda w: pl.BlockSpec((TM, w), lambda i: (i, 0))
    lay2 = lambda w: pl.BlockSpec((None, 1, w), lambda i: (l, 0, 0))
    lay3 = lambda a, b: pl.BlockSpec((None, a, b), lambda i: (l, 0, 0), pipeline_mode=pl.Buffered(1))
    if oa.ndim == 4:
        spb = oa.shape[2] // TM
        oa_spec = pl.BlockSpec((None, HA, TM, LANES), lambda i: (i // spb, 0, i % spb, 0))
    else:
        oa_spec = row(WA)
    return pl.pallas_call(
        functools.partial(_merge_kernel, lam_init=lam_init),
        grid=(t // TM,),
        in_specs=[row(D_MODEL), oa_spec, row(WGV), row(WGV), lay2(D_MODEL), lay2(2 * DHA), lay2(DVG),
                  lay3(2 * D_MODEL, D_MODEL), lay3(WA, D_MODEL), lay3(WGV, D_MODEL), lay3(D_MODEL, D_MODEL)],
        out_specs=row(D_MODEL),
        out_shape=jax.ShapeDtypeStruct((t, D_MODEL), F32),
        scratch_shapes=[pltpu.VMEM((TM, WA), BF16), pltpu.VMEM((TM, WGV), BF16)],
        compiler_params=_cparams(("parallel",)),
        name="merge",
    )(x, oa, og, rg, p["norm_mix"], p["subln"], p["gla_norm"],
      p["w_gate"], p["w_diff_out"], p["w_gla_out"], p["w_out"])


def _ffn_kernel(x_ref, g_ref, wg_ref, wu_ref, wd_ref, o_ref, *, chunk):
    x = x_ref[...]
    h = _rms(x, g_ref[...]).astype(BF16)
    acc = x
    for c0 in range(0, wg_ref.shape[1], chunk):
        a = jax.nn.silu(_dot(h, wg_ref[:, c0:c0 + chunk])) * _dot(h, wu_ref[:, c0:c0 + chunk])
        acc = acc + _dot(a.astype(BF16), wd_ref[c0:c0 + chunk, :])
    o_ref[...] = acc


def _ffn(x, l, p):
    t = x.shape[0]
    f = p["w_ffn_gate"].shape[2]
    row = pl.BlockSpec((TM, D_MODEL), lambda i: (i, 0))
    once = pl.Buffered(1)
    return pl.pallas_call(
        functools.partial(_ffn_kernel, chunk=f // 2),
        grid=(t // TM,),
        in_specs=[row, pl.BlockSpec((None, 1, D_MODEL), lambda i: (l, 0, 0)),
                  pl.BlockSpec((None, D_MODEL, f), lambda i: (l, 0, 0), pipeline_mode=once),
                  pl.BlockSpec((None, D_MODEL, f), lambda i: (l, 0, 0), pipeline_mode=once),
                  pl.BlockSpec((None, f, D_MODEL), lambda i: (l, 0, 0), pipeline_mode=once)],
        out_specs=row,
        out_shape=jax.ShapeDtypeStruct((t, D_MODEL), F32),
        compiler_params=_cparams(("parallel",)),
        name="ffn",
    )(x, p["norm_ffn"], p["w_ffn_gate"], p["w_ffn_up"], p["w_ffn_down"])


def kernel(x_prompt, x_sample, cache_k, cache_v, state_gla, page_table, rel_bias, norm_mix, w_in, w_alpha2, b_alpha, qn_gain, kn_gain, lam_q1, lam_k1, lam_q2, lam_k2, subln_gain, gla_norm_gain, w_diff_out, w_gla_out, w_out, norm_ffn, w_ffn_gate, w_ffn_up, w_ffn_down):
    batch, seq, _ = x_prompt.shape
    dec_batch, ts, _ = x_sample.shape
    n_pages = page_table.shape[1]
    n_pool = cache_k.shape[1]
    past = n_pages * PAGE_SIZE
    depth = w_in.shape[0]
    gate0 = MAIN_W + GATE_RANK

    vec = lambda a: a.reshape(depth, 1, a.shape[-1])
    w_in_t = jnp.transpose(w_in, (0, 2, 1)).astype(BF16)
    p = {
        "norm_mix": vec(norm_mix), "norm_ffn": vec(norm_ffn), "b_alpha": vec(b_alpha),
        "gq": vec(jnp.tile(qn_gain, (1, WA // DHA))), "gk": vec(jnp.tile(kn_gain, (1, WA // DHA))),
        "subln": vec(subln_gain), "gla_norm": vec(gla_norm_gain),
        "w_in_t": w_in_t,
        "w_gate": w_in_t[:, gate0:, :],
        "w_alpha2": w_alpha2.astype(BF16),
        "w_diff_out": w_diff_out.astype(BF16), "w_gla_out": w_gla_out.astype(BF16), "w_out": w_out.astype(BF16),
        "w_ffn_gate": w_ffn_gate.astype(BF16), "w_ffn_up": w_ffn_up.astype(BF16),
        "w_ffn_down": w_ffn_down.astype(BF16),
        "gsum": jnp.kron(jnp.eye(WA // DHA, dtype=F32), jnp.full((DHA, DHA), 1.0 / DHA, F32)).astype(BF16),
    }
    lamp = jnp.stack([lam_q1, lam_k1, lam_q2, lam_k2], axis=1)

    ckt = jnp.transpose(cache_k, (0, 1, 3, 4, 5, 2)).reshape(depth, n_pool, WA, PAGE_SIZE)
    cv4 = cache_v.reshape(depth, n_pool, PAGE_SIZE * HA, LANES)
    sg = state_gla.reshape(depth, dec_batch, HG // 2, 2 * DKG, DVG)
    s0p = jnp.zeros((1, batch, HG // 2, 2 * DKG, DVG), F32)

    bias_p = _bias_tables(rel_bias, 3, TK, TQ, TQ, 0, TK, True, True)
    bias_s = _bias_tables(rel_bias, 1, 2 * ts, past + PAGE_SIZE, ts, past, 0, False, False)
    bias_s = bias_s.reshape(HA * 2 * ts, past + PAGE_SIZE)
    flags = _logit_bound_flags(qn_gain, kn_gain, rel_bias)

    xp = x_prompt.reshape(batch * seq, D_MODEL)
    xs = x_sample.reshape(dec_batch * ts, D_MODEL)
    kv = ()
    kv_s = ()
    sp_all = ss_all = None
    for l in range(depth):
        lam_init = 0.8 - 0.6 * math.exp(-0.3 * l)
        qa, kt_all, v4_all, kb, vtb, qg, kg, vg, rg, la = _inproj(xp, l, p, True, kv, batch, seq)
        kv = (kt_all, v4_all)
        oa = _attn_prompt(qa, kb, vtb, bias_p, flags, lamp[l], l, lam_init, batch, seq)
        og, sp_all = _gla(qg, kg, vg, la, s0p, 0, sp_all, l, depth, batch, seq, batch, GLA_C, GLA_C)
        xp = _merge(xp, oa, og, rg, l, p, lam_init)
        xp = _ffn(xp, l, p)
        qa, ka, va, kst_all, vs4_all, qg, kg, vg, rg, la = _inproj(xs, l, p, False, kv_s, dec_batch, ts)
        kv_s = (kst_all, vs4_all)
        oa = _attn_sample(qa, ka, va, ckt, cv4, page_table, bias_s, lamp[l], l, lam_init, dec_batch, ts)
        og, ss_all = _gla(qg, kg, vg, la, sg, l, ss_all, l, depth, dec_batch, ts, GLA_SB, ts, 2 * ts)
        xs = _merge(xs, oa, og, rg, l, p, lam_init)
        xs = _ffn(xs, l, p)
    kt_all, v4_all = kv
    kst_all, vs4_all = kv_s
    k_prompt = jnp.transpose(kt_all.reshape(depth, batch, HA, 2, DHA, seq), (0, 1, 5, 2, 3, 4))
    v_prompt = v4_all.reshape(depth, batch, seq, HA, 2 * DHA)
    k_sample = jnp.transpose(kst_all.reshape(depth, ts, HA, 2, DHA, dec_batch), (0, 5, 1, 2, 3, 4))
    v_sample = vs4_all.reshape(depth, dec_batch, ts, HA, 2 * DHA)
    return (xp.reshape(batch, seq, D_MODEL), xs.reshape(dec_batch, ts, D_MODEL),
            k_prompt, v_prompt, sp_all.reshape(depth, batch, HG, DKG, DVG),
            k_sample, v_sample, ss_all.reshape(depth, dec_batch, HG, DKG, DVG))
```

```python
import functools
import math

import jax
import jax.numpy as jnp
from jax import lax
from jax.experimental import pallas as pl
from jax.experimental.pallas import tpu as pltpu

F32 = jnp.float32
BF16 = jnp.bfloat16

D_MODEL = 1024
HA = 4
DHA = 64
HG = 4
DKG = 64
DVG = 128
GATE_RANK = 16
GATE_NORM = 16.0
N_BUCKETS = 32
MAX_DISTANCE = 128
PAGE_SIZE = 128
EPS = 1e-6
WA = HA * 2 * DHA
WGK = HG * DKG
WGV = HG * DVG
MAIN_W = 3 * WA + 2 * WGK + 2 * WGV
LANES = 128
NEG = -1e30
LOG2E = math.log2(math.e)
GLA_RANGE_SAFE = 80.0
LOGIT_SAFE = 60.0

TM = 512
TQ = 512
TK = 512
GLA_C = 64
GLA_SUB = 2
GLA_SB = 8
VMEM_LIMIT = 56 * 1024 * 1024


def _cparams(sem):
    return pltpu.CompilerParams(dimension_semantics=sem, vmem_limit_bytes=VMEM_LIMIT)


def _rms(x, gain):
    ms = jnp.mean(x * x, axis=-1, keepdims=True)
    return x * lax.rsqrt(ms + EPS) * gain


def _dot(a, b):
    return jnp.dot(a, b, preferred_element_type=F32)


def _dot_nt(a, b):
    return lax.dot_general(a, b, (((1,), (1,)), ((), ())), preferred_element_type=F32)


def _dot_tn(a, b):
    return lax.dot_general(a, b, (((0,), (0,)), ((), ())), preferred_element_type=F32)


def _lam(lamp_ref, lam_init):
    a = jnp.sum(lamp_ref[0:1, :] * lamp_ref[1:2, :], axis=-1, keepdims=True)
    b = jnp.sum(lamp_ref[2:3, :] * lamp_ref[3:4, :], axis=-1, keepdims=True)
    return jnp.exp(a) - jnp.exp(b) + lam_init


def _bias_kernel(rb_ref, o_ref, *, period, off0, off_step, transposed, shifted):
    h = pl.program_id(0)
    t = pl.program_id(1)
    rows, cols = o_ref.shape
    r = lax.broadcasted_iota(jnp.int32, (rows, cols), 0)
    c = lax.broadcasted_iota(jnp.int32, (rows, cols), 1)
    if transposed:
        r, c = c, r
    d = off0 + t * off_step + jnp.bitwise_and(r, period - 1) - c
    n = jnp.maximum(d, 0)
    max_exact = N_BUCKETS // 2
    nf = jnp.maximum(n, 1).astype(F32)
    large = max_exact + (jnp.log(nf / max_exact) / math.log(MAX_DISTANCE / max_exact)
                         * (N_BUCKETS - max_exact)).astype(jnp.int32)
    large = jnp.minimum(large, N_BUCKETS - 1)
    bucket = jnp.where(n < max_exact, n, large)
    val = jnp.zeros((rows, cols), F32)
    for k in range(N_BUCKETS):
        val = jnp.where(bucket == k, rb_ref[k, h], val)
    if shifted:
        val = val - rb_ref[N_BUCKETS - 1, h]
    o_ref[...] = jnp.where(d >= 0, val * LOG2E, NEG)


def _bias_tables(rel_bias, n_t, rows, cols, period, off0, off_step, transposed, shifted):
    return pl.pallas_call(
        functools.partial(_bias_kernel, period=period, off0=off0, off_step=off_step,
                          transposed=transposed, shifted=shifted),
        grid=(HA, n_t),
        in_specs=[pl.BlockSpec(memory_space=pltpu.SMEM)],
        out_specs=pl.BlockSpec((None, None, rows, cols), lambda h, t: (h, t, 0, 0)),
        out_shape=jax.ShapeDtypeStruct((HA, n_t, rows, cols), F32),
        compiler_params=_cparams(("arbitrary", "arbitrary")),
        name="bias_tables",
    )(rel_bias)


def _inproj_kernel(*refs, prompt, n_alias, ts):
    (x_ref, gmix_ref, w_ref, walr_ref, wa2_ref, ba_ref, gq_ref, gk_ref, gsum_ref) = refs[:9]
    outs = refs[9 + n_alias:]
    if prompt:
        (qa_ref, ka_ref, va_ref, kb_ref, vtb_ref, qg_ref, kg_ref, vg_ref, rg_ref, la_ref) = outs
    else:
        (qa_ref, ka_ref, va_ref, kst_ref, vs4_ref, qg_ref, kg_ref, vg_ref, rg_ref, la_ref, kscr_ref) = outs
    h = _rms(x_ref[...], gmix_ref[...]).astype(BF16)
    proj = _dot_nt(h, w_ref[...])
    gs = gsum_ref[...]
    qa = proj[:, 0:WA]
    ka = proj[:, WA:2 * WA]
    qms = _dot((qa * qa).astype(BF16), gs)
    kms = _dot((ka * ka).astype(BF16), gs)
    qan = (qa * lax.rsqrt(qms + EPS) * gq_ref[...] * (DHA ** -0.5 * LOG2E)).astype(BF16)
    if prompt:
        for hd in range(HA):
            qa_ref[hd] = qan[:, hd * LANES:(hd + 1) * LANES]
    else:
        qa_ref[...] = qan
    kan = ka * lax.rsqrt(kms + EPS) * gk_ref[...]
    if prompt:
        ka_ref[...] = kan.T
        for hd in range(HA):
            kb_ref[hd] = kan[:, hd * LANES:(hd + 1) * LANES].astype(BF16)
        vtb_ref[...] = proj[:, 2 * WA:3 * WA].T.astype(BF16)
        for hd in range(HA):
            va_ref[pl.ds(hd, TM, stride=HA), :] = proj[:, 2 * WA + hd * LANES:2 * WA + (hd + 1) * LANES]
    else:
        ka_ref[...] = kan
        va_ref[...] = proj[:, 2 * WA:3 * WA]
        for hd in range(HA):
            vs4_ref[pl.ds(hd, TM, stride=HA), :] = proj[:, 2 * WA + hd * LANES:2 * WA + (hd + 1) * LANES]
        for j in range(WA // LANES):
            kscr_ref[j] = kan[:, j * LANES:(j + 1) * LANES]
        nbt = TM // ts
        for tile in range(kst_ref.shape[2] // nbt):
            @pl.when(pl.program_id(0) == tile)
            def _(tile=tile):
                lo, hi = tile * nbt, (tile + 1) * nbt
                for t in range(ts):
                    for j in range(WA // LANES):
                        rows = kscr_ref[j, pl.ds(t, nbt, stride=ts), :]
                        parts = [jnp.zeros((n, LANES), F32) for n in (lo,) if n] + [rows]
                        parts += [jnp.zeros((n, LANES), F32) for n in (LANES - hi,) if n]
                        tr = jnp.concatenate(parts, axis=0).T
                        kst_ref[t, j * LANES:(j + 1) * LANES, lo:hi] = tr[:, lo:hi]
    o = 3 * WA
    qg_ref[...] = proj[:, o:o + WGK] * (DKG ** -0.5)
    kg_ref[...] = proj[:, o + WGK:o + 2 * WGK]
    o += 2 * WGK
    vg_ref[...] = proj[:, o:o + WGV]
    rg_ref[...] = proj[:, o + WGV:o + 2 * WGV]
    alr = _dot_nt(h, walr_ref[...])
    z = _dot(alr.astype(BF16), wa2_ref[...]) + ba_ref[...]
    log_sig = jnp.minimum(z, 0.0) - jnp.log(1.0 + jnp.exp(-jnp.abs(z)))
    la_ref[...] = log_sig * (1.0 / GATE_NORM)


def _inproj(x, l, p, prompt, kv_prev, batch, seq):
    t = x.shape[0]
    depth = p["w_in_t"].shape[0]
    row = lambda w: pl.BlockSpec((TM, w), lambda i: (i, 0))
    lay2 = lambda w: pl.BlockSpec((None, 1, w), lambda i: (l, 0, 0))
    lay3 = lambda a, b: pl.BlockSpec((None, a, b), lambda i: (l, 0, 0), pipeline_mode=pl.Buffered(1))
    in_specs = [row(D_MODEL), lay2(D_MODEL), lay3(MAIN_W, D_MODEL),
                pl.BlockSpec((None, GATE_RANK, D_MODEL), lambda i: (l, MAIN_W // GATE_RANK, 0)),
                lay3(GATE_RANK, WGK), lay2(WGK), lay2(WA), lay2(WA),
                pl.BlockSpec((WA, WA), lambda i: (0, 0))]
    args = [x, p["norm_mix"], p["w_in_t"], p["w_in_t"], p["w_alpha2"], p["b_alpha"], p["gq"], p["gk"], p["gsum"]]
    rest = [(WGK, F32), (WGK, F32), (WGV, F32), (WGV, F32), (WGK, F32)]
    if prompt:
        spb = seq // TM
        heads = pl.BlockSpec((None, HA, TM, LANES), lambda i: (i // spb, 0, i % spb, 0))
        heads_shape = jax.ShapeDtypeStruct((batch, HA, seq, LANES), BF16)
        kv_specs = [pl.BlockSpec((None, None, WA, TM), lambda i: (l, i // spb, 0, i % spb)),
                    pl.BlockSpec((None, TM * HA, LANES), lambda i: (l, i, 0)),
                    heads,
                    pl.BlockSpec((None, WA, TM), lambda i: (i // spb, 0, i % spb))]
        kv_shapes = [jax.ShapeDtypeStruct((depth, batch, WA, seq), F32),
                     jax.ShapeDtypeStruct((depth, t * HA, LANES), F32),
                     heads_shape,
                     jax.ShapeDtypeStruct((batch, WA, seq), BF16)]
        q_spec, q_shape = heads, heads_shape
        first_kv = 1
        scratch = []
    else:
        kv_specs = [row(WA), row(WA),
                    pl.BlockSpec((None, seq, WA, batch), lambda i: (l, 0, 0, 0)),
                    pl.BlockSpec((None, TM * HA, LANES), lambda i: (l, i, 0))]
        kv_shapes = [jax.ShapeDtypeStruct((t, WA), F32)] * 2 + [
            jax.ShapeDtypeStruct((depth, seq, WA, batch), F32),
            jax.ShapeDtypeStruct((depth, t * HA, LANES), F32)]
        q_spec, q_shape = row(WA), jax.ShapeDtypeStruct((t, WA), BF16)
        first_kv = 3
        scratch = [pltpu.VMEM((WA // LANES, TM, LANES), F32)]
    aliases = {}
    for n, a in enumerate(kv_prev):
        in_specs.append(pl.BlockSpec(memory_space=pl.ANY))
        args.append(a)
        aliases[9 + n] = first_kv + n
    return pl.pallas_call(
        functools.partial(_inproj_kernel, prompt=prompt, n_alias=len(aliases), ts=seq),
        grid=(t // TM,),
        in_specs=in_specs,
        out_specs=[q_spec] + kv_specs + [row(w) for w, _ in rest],
        out_shape=[q_shape] + kv_shapes
                  + [jax.ShapeDtypeStruct((t, w), dt) for w, dt in rest],
        scratch_shapes=scratch,
        input_output_aliases=aliases,
        compiler_params=_cparams(("parallel",) if prompt else ("arbitrary",)),
        name="inproj",
    )(*args)


def _bound_kernel(gq_ref, gk_ref, rb_ref, o_ref):
    mq = jnp.max(jnp.abs(gq_ref[...]), axis=1, keepdims=True)
    mk = jnp.max(jnp.abs(gk_ref[...]), axis=1, keepdims=True)
    rb = rb_ref[...]
    shifted = jnp.abs(rb - rb[N_BUCKETS - 1:N_BUCKETS, :])
    bmax = jnp.max(jnp.max(shifted, axis=1, keepdims=True), axis=0, keepdims=True)
    bound = LOG2E * (1.01 * DHA ** 0.5 * mq * mk + bmax)
    o_ref[...] = jnp.broadcast_to((bound <= LOGIT_SAFE).astype(jnp.int32), o_ref.shape)


def _logit_bound_flags(qn_gain, kn_gain, rel_bias):
    depth = qn_gain.shape[0]
    return pl.pallas_call(
        _bound_kernel,
        out_shape=jax.ShapeDtypeStruct((depth, LANES), jnp.int32),
        name="logit_bound",
    )(qn_gain, kn_gain, rel_bias)


def _attn_kernel(flag_ref, lamp_ref, q_ref, kb_ref, vtb_ref, bias_ref, o_ref,
                 qs_ref, m_ref, l_ref, acc_ref, sa_ref, sb_ref, *, lam_init, seq, layer):
    nq = seq // TQ
    bounded = flag_ref[layer, 0] == 1

    lane = lax.broadcasted_iota(jnp.int32, (1, LANES), 1)
    lam = _lam(lamp_ref, lam_init)

    def logits(j, dst_ref):
        start = pl.multiple_of(j * TK, TK)
        dst_ref[...] = _dot_nt(kb_ref[pl.ds(start, TK), :], qs_ref[...])

    def start_block(i):
        q = q_ref[pl.ds(pl.multiple_of(i * TQ, TQ), TQ), :]
        qs_ref[0:TQ, :] = jnp.where(lane < DHA, q, jnp.zeros_like(q))
        qs_ref[TQ:2 * TQ, :] = jnp.where(lane >= DHA, q, jnp.zeros_like(q))
        m_ref[...] = jnp.full(m_ref.shape, NEG, F32)
        l_ref[...] = jnp.zeros(l_ref.shape, F32)
        acc_ref[...] = jnp.zeros(acc_ref.shape, F32)
        logits(jnp.where(bounded, i, 0), sa_ref)

    start_block(0)


    def biased(src_ref, table):
        b = bias_ref[table]
        return jnp.concatenate([src_ref[:, 0:TQ] + b, src_ref[:, TQ:2 * TQ] + b], axis=1)

    def accumulate(p, j):
        start = pl.multiple_of(j * TK, TK)
        l_ref[...] += jnp.sum(p, axis=0, keepdims=True)
        acc_ref[...] += _dot(vtb_ref[:, pl.ds(start, TK)], p.astype(BF16))

    def q_block(i, carry):
        jl = i

        @pl.when(bounded)
        def _():
            @pl.when(jl == 0)
            def _():
                accumulate(jnp.exp2(biased(sa_ref, 0)), jl)

            @pl.when(jl >= 1)
            def _():
                logits(jl - 1, sb_ref)
                accumulate(jnp.exp2(biased(sa_ref, 0)), jl)
                logits(0, sa_ref)
                accumulate(jnp.exp2(biased(sb_ref, 1)), jl - 1)

            n_far = jnp.maximum(jl - 1, 0)

            def far_pair(jj, carry):
                j0 = 2 * jj
                logits(jnp.minimum(j0 + 1, jl), sb_ref)
                accumulate(jnp.exp2(sa_ref[...]), j0)

                @pl.when(j0 + 1 < n_far)
                def _():
                    logits(jnp.minimum(j0 + 2, jl), sa_ref)
                    accumulate(jnp.exp2(sb_ref[...]), j0 + 1)

                return carry

            lax.fori_loop(0, (n_far + 1) // 2, far_pair, 0)

        @pl.when(jnp.logical_not(bounded))
        def _():
            def update(src_ref, j):
                table = jnp.minimum(jl - j, 2)
                start = pl.multiple_of(j * TK, TK)
                s = biased(src_ref, table)
                m_prev = m_ref[...]
                m_new = jnp.maximum(m_prev, jnp.max(s, axis=0, keepdims=True))
                alpha = jnp.exp2(m_prev - m_new)
                p = jnp.exp2(s - m_new)
                l_ref[...] = alpha * l_ref[...] + jnp.sum(p, axis=0, keepdims=True)
                acc_ref[...] = alpha * acc_ref[...] + _dot(vtb_ref[:, pl.ds(start, TK)], p.astype(BF16))
                m_ref[...] = m_new

            def pair_body(jj, carry):
                j0 = 2 * jj
                logits(jnp.minimum(j0 + 1, jl), sb_ref)
                update(sa_ref, j0)

                @pl.when(j0 + 1 <= jl)
                def _():
                    logits(jnp.minimum(j0 + 2, jl), sa_ref)
                    update(sb_ref, j0 + 1)

                return carry

            lax.fori_loop(0, jl // 2 + 1, pair_body, 0)

        o = acc_ref[...] * (1.0 / l_ref[...])
        o_ref[pl.ds(pl.multiple_of(i * TQ, TQ), TQ), :] = (o[:, 0:TQ] - lam * o[:, TQ:2 * TQ]).T
        start_block(jnp.minimum(i + 1, nq - 1))
        return carry

    lax.fori_loop(0, nq, q_block, 0)


def _attn_prompt(qa, kb, vtb, bias, flags, lamp, l, lam_init, batch, seq):
    blk = pl.BlockSpec((None, seq, LANES), lambda b, h: (b, 0, h))
    head_blk = pl.BlockSpec((None, None, seq, LANES), lambda b, h: (b, h, 0, 0))
    out = pl.pallas_call(
        functools.partial(_attn_kernel, lam_init=lam_init, seq=seq, layer=l),
        grid=(batch, HA),
        in_specs=[pl.BlockSpec(memory_space=pltpu.SMEM),
                  pl.BlockSpec((4, DHA), lambda b, h: (0, 0)),
                  head_blk, head_blk,
                  pl.BlockSpec((None, LANES, seq), lambda b, h: (b, h, 0)),
                  pl.BlockSpec((None, 3, TK, TQ), lambda b, h: (h, 0, 0, 0))],
        out_specs=head_blk,
        out_shape=jax.ShapeDtypeStruct((batch, HA, seq, LANES), F32),
        scratch_shapes=[pltpu.VMEM((2 * TQ, LANES), BF16),
                        pltpu.VMEM((1, 2 * TQ), F32), pltpu.VMEM((1, 2 * TQ), F32),
                        pltpu.VMEM((LANES, 2 * TQ), F32),
                        pltpu.VMEM((TK, 2 * TQ), F32), pltpu.VMEM((TK, 2 * TQ), F32)],
        compiler_params=_cparams(("parallel", "parallel")),
        name="attn_prompt",
    )(flags, lamp, qa, kb, vtb, bias)
    return out


def _attn_sample_kernel(pt_ref, lamp_ref, q_ref, kn_ref, vn_ref, bias_ref, *rest, n_pages, ts, lam_init):
    kp = rest[:n_pages]
    vp = rest[n_pages:2 * n_pages]
    o_ref = rest[2 * n_pages]
    s_ref = rest[2 * n_pages + 1]
    nr = HA * 2 * ts
    past = n_pages * PAGE_SIZE

    q = q_ref[...].astype(F32)
    qt = jnp.concatenate([q] * (2 * HA), axis=0)
    r = lax.broadcasted_iota(jnp.int32, (nr, WA), 0)
    c = lax.broadcasted_iota(jnp.int32, (nr, WA), 1)
    keep = lax.shift_right_logical(c, int(math.log2(DHA))) == lax.shift_right_logical(r, int(math.log2(ts)))
    qbd = jnp.where(keep, qt, 0.0).astype(BF16)
    zpad = jnp.zeros((PAGE_SIZE - ts, WA), F32)
    knp = jnp.concatenate([kn_ref[...], zpad], axis=0).astype(BF16)
    vnp = jnp.concatenate([vn_ref[...], zpad], axis=0).astype(BF16)

    for pg in range(n_pages):
        s_ref[:, pg * PAGE_SIZE:(pg + 1) * PAGE_SIZE] = _dot(qbd, kp[pg][...].astype(BF16))
    s_ref[:, past:past + PAGE_SIZE] = _dot_nt(qbd, knp)

    s = s_ref[...] + bias_ref[...]
    m = jnp.max(s, axis=-1, keepdims=True)
    p = jnp.exp2(s - m)
    l = jnp.sum(p, axis=-1, keepdims=True)
    pb = p.astype(BF16)
    lam = _lam(lamp_ref, lam_init)
    for h in range(HA):
        rs = slice(h * 2 * ts, (h + 1) * 2 * ts)
        cs = slice(h * LANES, (h + 1) * LANES)
        acc = _dot(pb[rs, past:past + PAGE_SIZE], vnp[:, cs])
        for pg in range(n_pages):
            vh = vp[pg][pl.ds(h, PAGE_SIZE, stride=HA), :].astype(BF16)
            acc = acc + _dot(pb[rs, pg * PAGE_SIZE:(pg + 1) * PAGE_SIZE], vh)
        acc = acc / l[rs]
        o_ref[:, cs] = acc[0:ts, :] - lam * acc[ts:2 * ts, :]


def _attn_sample(qa, ka, va, ckt, cv4, page_table, bias, lamp, l, lam_init, dec_batch, ts):
    n_pages = page_table.shape[1]
    nr = HA * 2 * ts
    width = n_pages * PAGE_SIZE + PAGE_SIZE
    tok = pl.BlockSpec((None, ts, WA), lambda b, pt: (b, 0, 0))
    page = lambda pg: pl.BlockSpec((None, None, HA * PAGE_SIZE, LANES), lambda b, pt: (l, pt[b, pg], 0, 0))
    grid_spec = pltpu.PrefetchScalarGridSpec(
        num_scalar_prefetch=1,
        grid=(dec_batch,),
        in_specs=[pl.BlockSpec((4, DHA), lambda b, pt: (0, 0)), tok, tok, tok,
                  pl.BlockSpec((nr, width), lambda b, pt: (0, 0))]
                 + [page(pg) for pg in range(n_pages)] * 2,
        out_specs=tok,
        scratch_shapes=[pltpu.VMEM((nr, width), F32)],
    )
    out = pl.pallas_call(
        functools.partial(_attn_sample_kernel, n_pages=n_pages, ts=ts, lam_init=lam_init),
        grid_spec=grid_spec,
        out_shape=jax.ShapeDtypeStruct((dec_batch, ts, WA), F32),
        compiler_params=_cparams(("parallel",)),
        name="attn_sample",
    )(page_table, lamp, qa.reshape(dec_batch, ts, WA), ka.reshape(dec_batch, ts, WA),
      va.reshape(dec_batch, ts, WA), bias, *([ckt] * n_pages), *([cv4] * n_pages))
    return out.reshape(dec_batch * ts, WA)


def _split3(x):
    x1 = x.astype(BF16)
    r1 = x - x1.astype(F32)
    x2 = r1.astype(BF16)
    x3 = (r1 - x2.astype(F32)).astype(BF16)
    return x1, x2, x3


def _gla_kernel(q_ref, k_ref, v_ref, g_ref, s0_ref, tril_ref, *rest, nb, c_in, c, n_sub):
    o_ref, sout_ref, st_scr = rest[-3:]
    ci = pl.program_id(1)
    units = [(b, pr) for b in range(nb) for pr in range(HG // 2)]

    @pl.when(ci == 0)
    def _():
        for b, pr in units:
            st_scr[b, pr] = s0_ref[b, pr].T

    def pad(x):
        if c_in == c:
            return x
        return jnp.concatenate([x, jnp.zeros((c - c_in, x.shape[1]), x.dtype)], axis=0)

    lane = lax.broadcasted_iota(jnp.int32, (1, LANES), 1)
    head0 = lane < DKG
    tt = lax.broadcasted_iota(jnp.int32, (2 * c, c), 0)
    ss = lax.broadcasted_iota(jnp.int32, (2 * c, c), 1)
    causal = jnp.bitwise_and(tt, c - 1) >= ss
    mid = c // 2 - 1

    def heads_on_rows(x):
        return jnp.concatenate([jnp.where(head0, x, 0.0), jnp.where(head0, 0.0, x)], axis=0).astype(BF16)

    for sub in range(n_sub):
        _gla_chunk(q_ref, k_ref, v_ref, g_ref, tril_ref, o_ref, st_scr, units, sub * c_in,
                   nb, c_in, c, pad, head0, causal, mid, heads_on_rows)

    @pl.when(ci == pl.num_programs(1) - 1)
    def _():
        for b, pr in units:
            sout_ref[b, pr] = st_scr[b, pr].T


def _gla_chunk(q_ref, k_ref, v_ref, g_ref, tril_ref, o_ref, st_scr, units, r0,
               nb, c_in, c, pad, head0, causal, mid, heads_on_rows):
    rows = slice(r0, r0 + c_in)

    g_all = jnp.concatenate([pad(g_ref[b, rows, :]) for b in range(nb)], axis=1)
    tril = tril_ref[...]
    g3 = _split3(g_all)
    bc_all = _dot(tril, g3[0]) + _dot(tril, g3[1]) + _dot(tril, g3[2])

    chunk_range = jnp.max(-bc_all[c - 1:c, :])

    @pl.when(chunk_range <= GLA_RANGE_SAFE)
    def _():
        prep = []
        for b, pr in units:
            ks = slice(pr * LANES, (pr + 1) * LANES)
            q = pad(q_ref[b, rows, ks])
            k = pad(k_ref[b, rows, ks])
            bcum = bc_all[:, b * WGK + pr * LANES:b * WGK + (pr + 1) * LANES]
            bm = bcum[mid:mid + 1, :]
            bl = bcum[c - 1:c, :]
            prep.append(dict(
                qt=heads_on_rows(q * jnp.exp(bcum - bm)),
                qi=heads_on_rows(q * jnp.exp(bcum)),
                kt=(k * jnp.exp(bm - bcum)).astype(BF16),
                kd=(k * jnp.exp(bl - bcum)).astype(BF16),
                decay=jnp.exp(bl),
                v2=pad(v_ref[b, rows, 2 * pr * DVG:2 * (pr + 1) * DVG]).astype(BF16)))

        amat = [jnp.where(causal, _dot_nt(u["qt"], u["kt"]), 0.0).astype(BF16) for u in prep]
        inter = [_dot_nt(u["qi"], st_scr[b, pr].astype(BF16)) for u, (b, pr) in zip(prep, units)]
        for u, a, it, (b, pr) in zip(prep, amat, inter, units):
            for hh in range(2):
                o = (_dot(a[hh * c:(hh + 1) * c, :], u["v2"][:, hh * DVG:(hh + 1) * DVG])
                     + it[hh * c:(hh + 1) * c, :])
                o_ref[b, rows, (2 * pr + hh) * DVG:(2 * pr + hh + 1) * DVG] = o[0:c_in, :]
        for u, (b, pr) in zip(prep, units):
            upd = _dot_tn(u["v2"], u["kd"])
            st_scr[b, pr] = st_scr[b, pr] * u["decay"] + jnp.where(head0, upd[0:DVG, :], upd[DVG:2 * DVG, :])

    @pl.when(jnp.logical_not(chunk_range <= GLA_RANGE_SAFE))
    def _():
        row_head0 = lax.broadcasted_iota(jnp.int32, (LANES, 1), 0) < DKG

        def columns(x):
            return jnp.concatenate([x, jnp.zeros((LANES - c_in, LANES), F32)], axis=0).T

        def unit_body(u, carry):
            b = u // (HG // 2)
            pr = u % (HG // 2)
            ks = pl.ds(pl.multiple_of(pr * LANES, LANES), LANES)
            qc = columns(q_ref[b, rows, ks])
            kc = columns(k_ref[b, rows, ks])
            ac = columns(jnp.exp(g_ref[b, rows, ks]))
            v2 = v_ref[b, rows, pl.ds(pl.multiple_of(pr * 2 * DVG, 2 * DVG), 2 * DVG)]
            s = st_scr[b, pr].T
            for t in range(c_in):
                vrow = jnp.where(row_head0, v2[t:t + 1, 0:DVG], v2[t:t + 1, DVG:2 * DVG])
                s = s * ac[:, t:t + 1] + kc[:, t:t + 1] * vrow
                w = qc[:, t:t + 1] * s
                o_ref[b, pl.ds(r0 + t, 1), pl.ds(pl.multiple_of(pr * 2 * DVG, 2 * DVG), DVG)] = (
                    jnp.sum(w[0:DKG, :], axis=0, keepdims=True))
                o_ref[b, pl.ds(r0 + t, 1), pl.ds(pl.multiple_of(pr * 2 * DVG + DVG, DVG), DVG)] = (
                    jnp.sum(w[DKG:2 * DKG, :], axis=0, keepdims=True))
            st_scr[b, pr] = s.T
            return carry

        lax.fori_loop(0, len(units), unit_body, 0)


def _gla(qg, kg, vg, la, s0, s0_layer, s_prev, layer, depth, batch, seq, nb, c_in, c, n_sub):
    n_chunks = seq // (c_in * n_sub)
    tril = jnp.tril(jnp.ones((c, c), BF16))
    tok = lambda w: pl.BlockSpec((nb, c_in * n_sub, w), lambda bi, ci: (bi, ci, 0))
    st_out = pl.BlockSpec((None, nb, HG // 2, 2 * DKG, DVG), lambda bi, ci: (layer, bi, 0, 0, 0))
    st_in = pl.BlockSpec((None, nb, HG // 2, 2 * DKG, DVG), lambda bi, ci: (s0_layer, bi, 0, 0, 0))
    in_specs = [tok(WGK), tok(WGK), tok(WGV), tok(WGK), st_in, pl.BlockSpec((c, c), lambda bi, ci: (0, 0))]
    args = [qg.reshape(batch, seq, WGK), kg.reshape(batch, seq, WGK), vg.reshape(batch, seq, WGV),
            la.reshape(batch, seq, WGK), s0, tril]
    aliases = {}
    if s_prev is not None:
        in_specs.append(pl.BlockSpec(memory_space=pl.ANY))
        args.append(s_prev)
        aliases = {6: 1}
    o, s_all = pl.pallas_call(
        functools.partial(_gla_kernel, nb=nb, c_in=c_in, c=c, n_sub=n_sub),
        grid=(batch // nb, n_chunks),
        in_specs=in_specs,
        out_specs=[tok(WGV), st_out],
        out_shape=[jax.ShapeDtypeStruct((batch, seq, WGV), F32),
                   jax.ShapeDtypeStruct((depth, batch, HG // 2, 2 * DKG, DVG), F32)],
        scratch_shapes=[pltpu.VMEM((nb, HG // 2, 2 * DKG, DVG), F32)],
        input_output_aliases=aliases,
        compiler_params=_cparams(("parallel", "arbitrary")),
        name="gla",
    )(*args)
    return o.reshape(batch * seq, WGV), s_all


def _merge_kernel(x_ref, oa_ref, og_ref, rg_ref, gmix_ref, subln_ref, glan_ref,
                  wgate_ref, wdo_ref, wgo_ref, wout_ref, o_ref, ya_ref, yg_ref, *, lam_init):
    x = x_ref[...]
    h = _rms(x, gmix_ref[...]).astype(BF16)
    gate = jax.nn.sigmoid(_dot_nt(h, wgate_ref[...]))
    for hd in range(HA):
        cs = slice(hd * LANES, (hd + 1) * LANES)
        oa = oa_ref[hd] if len(oa_ref.shape) == 3 else oa_ref[:, cs]
        ya_ref[:, cs] = (_rms(oa, subln_ref[...]) * (1.0 - lam_init)).astype(BF16)
        yg_ref[:, cs] = (_rms(og_ref[:, cs], glan_ref[...]) * jax.nn.silu(rg_ref[:, cs])).astype(BF16)
    ya = _dot(ya_ref[...], wdo_ref[...])
    yg = _dot(yg_ref[...], wgo_ref[...])
    mix = gate[:, 0:D_MODEL] * ya + gate[:, D_MODEL:2 * D_MODEL] * yg
    o_ref[...] = x + _dot(mix.astype(BF16), wout_ref[...])


def _merge(x, oa, og, rg, l, p, lam_init):
    t = x.shape[0]
    row = lambda w: pl.BlockSpec((TM, w), lambda i: (i, 0))
    lay2 = lambda w: pl.BlockSpec((None, 1, w), lambda i: (l, 0, 0))
    lay3 = lambda a, b: pl.BlockSpec((None, a, b), lambda i: (l, 0, 0), pipeline_mode=pl.Buffered(1))
    if oa.ndim == 4:
        spb = oa.shape[2] // TM
        oa_spec = pl.BlockSpec((None, HA, TM, LANES), lambda i: (i // spb, 0, i % spb, 0))
    else:
        oa_spec = row(WA)
    return pl.pallas_call(
        functools.partial(_merge_kernel, lam_init=lam_init),
        grid=(t // TM,),
        in_specs=[row(D_MODEL), oa_spec, row(WGV), row(WGV), lay2(D_MODEL), lay2(2 * DHA), lay2(DVG),
                  lay3(2 * D_MODEL, D_MODEL), lay3(WA, D_MODEL), lay3(WGV, D_MODEL), lay3(D_MODEL, D_MODEL)],
        out_specs=row(D_MODEL),
        out_shape=jax.ShapeDtypeStruct((t, D_MODEL), F32),
        scratch_shapes=[pltpu.VMEM((TM, WA), BF16), pltpu.VMEM((TM, WGV), BF16)],
        compiler_params=_cparams(("parallel",)),
        name="merge",
    )(x, oa, og, rg, p["norm_mix"], p["subln"], p["gla_norm"],
      p["w_gate"], p["w_diff_out"], p["w_gla_out"], p["w_out"])


def _ffn_kernel(x_ref, g_ref, wg_ref, wu_ref, wd_ref, o_ref, *, chunk):
    x = x_ref[...]
    h = _rms(x, g_ref[...]).astype(BF16)
    acc = x
    for c0 in range(0, wg_ref.shape[1], chunk):
        a = jax.nn.silu(_dot(h, wg_ref[:, c0:c0 + chunk])) * _dot(h, wu_ref[:, c0:c0 + chunk])
        acc = acc + _dot(a.astype(BF16), wd_ref[c0:c0 + chunk, :])
    o_ref[...] = acc


def _ffn(x, l, p):
    t = x.shape[0]
    f = p["w_ffn_gate"].shape[2]
    row = pl.BlockSpec((TM, D_MODEL), lambda i: (i, 0))
    once = pl.Buffered(1)
    return pl.pallas_call(
        functools.partial(_ffn_kernel, chunk=f // 2),
        grid=(t // TM,),
        in_specs=[row, pl.BlockSpec((None, 1, D_MODEL), lambda i: (l, 0, 0)),
                  pl.BlockSpec((None, D_MODEL, f), lambda i: (l, 0, 0), pipeline_mode=once),
                  pl.BlockSpec((None, D_MODEL, f), lambda i: (l, 0, 0), pipeline_mode=once),
                  pl.BlockSpec((None, f, D_MODEL), lambda i: (l, 0, 0), pipeline_mode=once)],
        out_specs=row,
        out_shape=jax.ShapeDtypeStruct((t, D_MODEL), F32),
        compiler_params=_cparams(("parallel",)),
        name="ffn",
    )(x, p["norm_ffn"], p["w_ffn_gate"], p["w_ffn_up"], p["w_ffn_down"])


def kernel(x_prompt, x_sample, cache_k, cache_v, state_gla, page_table, rel_bias, norm_mix, w_in, w_alpha2, b_alpha, qn_gain, kn_gain, lam_q1, lam_k1, lam_q2, lam_k2, subln_gain, gla_norm_gain, w_diff_out, w_gla_out, w_out, norm_ffn, w_ffn_gate, w_ffn_up, w_ffn_down):
    batch, seq, _ = x_prompt.shape
    dec_batch, ts, _ = x_sample.shape
    n_pages = page_table.shape[1]
    n_pool = cache_k.shape[1]
    past = n_pages * PAGE_SIZE
    depth = w_in.shape[0]
    gate0 = MAIN_W + GATE_RANK

    vec = lambda a: a.reshape(depth, 1, a.shape[-1])
    w_in_t = jnp.transpose(w_in, (0, 2, 1)).astype(BF16)
    p = {
        "norm_mix": vec(norm_mix), "norm_ffn": vec(norm_ffn), "b_alpha": vec(b_alpha),
        "gq": vec(jnp.tile(qn_gain, (1, WA // DHA))), "gk": vec(jnp.tile(kn_gain, (1, WA // DHA))),
        "subln": vec(subln_gain), "gla_norm": vec(gla_norm_gain),
        "w_in_t": w_in_t,
        "w_gate": w_in_t[:, gate0:, :],
        "w_alpha2": w_alpha2.astype(BF16),
        "w_diff_out": w_diff_out.astype(BF16), "w_gla_out": w_gla_out.astype(BF16), "w_out": w_out.astype(BF16),
        "w_ffn_gate": w_ffn_gate.astype(BF16), "w_ffn_up": w_ffn_up.astype(BF16),
        "w_ffn_down": w_ffn_down.astype(BF16),
        "gsum": jnp.kron(jnp.eye(WA // DHA, dtype=F32), jnp.full((DHA, DHA), 1.0 / DHA, F32)).astype(BF16),
    }
    lamp = jnp.stack([lam_q1, lam_k1, lam_q2, lam_k2], axis=1)

    ckt = jnp.transpose(cache_k, (0, 1, 3, 4, 5, 2)).reshape(depth, n_pool, WA, PAGE_SIZE)
    cv4 = cache_v.reshape(depth, n_pool, PAGE_SIZE * HA, LANES)
    sg = state_gla.reshape(depth, dec_batch, HG // 2, 2 * DKG, DVG)
    s0p = jnp.zeros((1, batch, HG // 2, 2 * DKG, DVG), F32)

    bias_p = _bias_tables(rel_bias, 3, TK, TQ, TQ, 0, TK, True, True)
    bias_s = _bias_tables(rel_bias, 1, 2 * ts, past + PAGE_SIZE, ts, past, 0, False, False)
    bias_s = bias_s.reshape(HA * 2 * ts, past + PAGE_SIZE)
    flags = _logit_bound_flags(qn_gain, kn_gain, rel_bias)

    xp = x_prompt.reshape(batch * seq, D_MODEL)
    xs = x_sample.reshape(dec_batch * ts, D_MODEL)
    kv = ()
    kv_s = ()
    sp_all = ss_all = None
    for l in range(depth):
        lam_init = 0.8 - 0.6 * math.exp(-0.3 * l)
        qa, kt_all, v4_all, kb, vtb, qg, kg, vg, rg, la = _inproj(xp, l, p, True, kv, batch, seq)
        kv = (kt_all, v4_all)
        oa = _attn_prompt(qa, kb, vtb, bias_p, flags, lamp[l], l, lam_init, batch, seq)
        og, sp_all = _gla(qg, kg, vg, la, s0p, 0, sp_all, l, depth, batch, seq, batch, GLA_C, GLA_C, GLA_SUB)
        xp = _merge(xp, oa, og, rg, l, p, lam_init)
        xp = _ffn(xp, l, p)
        qa, ka, va, kst_all, vs4_all, qg, kg, vg, rg, la = _inproj(xs, l, p, False, kv_s, dec_batch, ts)
        kv_s = (kst_all, vs4_all)
        oa = _attn_sample(qa, ka, va, ckt, cv4, page_table, bias_s, lamp[l], l, lam_init, dec_batch, ts)
        og, ss_all = _gla(qg, kg, vg, la, sg, l, ss_all, l, depth, dec_batch, ts, GLA_SB, ts, 2 * ts, 1)
        xs = _merge(xs, oa, og, rg, l, p, lam_init)
        xs = _ffn(xs, l, p)
    kt_all, v4_all = kv
    kst_all, vs4_all = kv_s
    k_prompt = jnp.transpose(kt_all.reshape(depth, batch, HA, 2, DHA, seq), (0, 1, 5, 2, 3, 4))
    v_prompt = v4_all.reshape(depth, batch, seq, HA, 2 * DHA)
    k_sample = jnp.transpose(kst_all.reshape(depth, ts, HA, 2, DHA, dec_batch), (0, 5, 1, 2, 3, 4))
    v_sample = vs4_all.reshape(depth, dec_batch, ts, HA, 2 * DHA)
    return (xp.reshape(batch, seq, D_MODEL), xs.reshape(dec_batch, ts, D_MODEL),
            k_prompt, v_prompt, sp_all.reshape(depth, batch, HG, DKG, DVG),
            k_sample, v_sample, ss_all.reshape(depth, dec_batch, HG, DKG, DVG))
```
